```python
import math, functools
import jax, jax.numpy as jnp
from jax import lax
import numpy as np

D_MODEL = 2048
BATCH = 2
SEQ = 4096
DEPTH = 2
DEC_BATCH = 128
DEC_SEQ = 8
PAST_LEN = 2048
PAGE_SIZE = 128

A_HEADS = 8
A_SUB = D_MODEL // 32
A_VDIM = 2 * A_SUB
A_WIDTH = A_HEADS * A_VDIM
A_SCALE = A_SUB ** -0.5
B_HEADS = 4
B_DK = D_MODEL // 16
B_DV = B_DK
B_WIDTH = B_HEADS * B_DV
C_WIDTH = D_MODEL - A_WIDTH - B_WIDTH
C_BLOCKS = 4
C_BLOCK = C_WIDTH // C_BLOCKS
CONV_W = 4
RG_C = 8.0
HGRN_CHUNK = 64
Q_BLOCK = 128
N_BUCKETS = 32
MAX_EXACT = N_BUCKETS // 2
MAX_DISTANCE = 128
FF_DIM = D_MODEL * 11 // 4
N_EXPERTS = 8
TOP_K = 2
N_DENSE = (DEPTH + 1) // 2
N_MOE = DEPTH // 2
EPS = 1e-6
SPLITS = (A_WIDTH, 2 * A_WIDTH, 3 * A_WIDTH,
          3 * A_WIDTH + B_WIDTH, 3 * A_WIDTH + 2 * B_WIDTH, 3 * A_WIDTH + 3 * B_WIDTH,
          3 * A_WIDTH + 4 * B_WIDTH, 3 * A_WIDTH + 4 * B_WIDTH + C_WIDTH)
IN_COLS = 3 * A_WIDTH + 4 * B_WIDTH + 2 * C_WIDTH

kernel_name = "hymba_diffattn_hgrn2_rglru_moe_step"


def rmsnorm(x, g):
    x32 = x.astype(jnp.float32)
    y = x32 * lax.rsqrt(jnp.mean(x32 * x32, axis=-1, keepdims=True) + EPS)
    return (y * g.astype(jnp.float32)).astype(x.dtype)


def rel_bias(table, q_pos, k_pos):
    n = jnp.maximum(q_pos[:, None] - k_pos[None, :], 0)
    nf = jnp.maximum(n, 1).astype(jnp.float32)
    far = MAX_EXACT + (jnp.log(nf / MAX_EXACT) / math.log(MAX_DISTANCE / MAX_EXACT)
                       * (N_BUCKETS - MAX_EXACT)).astype(jnp.int32)
    bucket = jnp.where(n < MAX_EXACT, n, jnp.minimum(far, N_BUCKETS - 1))
    return jnp.moveaxis(table[bucket].astype(jnp.float32), -1, 0)


def diff_weights(logits, lam):
    p = jax.nn.softmax(logits, axis=-1)
    return p[:, :, 0] - lam * p[:, :, 1]


def attn_prompt(q, k, v, lam, table):
    B, S = q.shape[0], q.shape[1]
    nb = S // Q_BLOCK
    qb = jnp.moveaxis(q.reshape(B, nb, Q_BLOCK, A_HEADS, 2, A_SUB), 1, 0)
    k_pos = jnp.arange(S)

    def block(args):
        q_blk, start = args
        q_pos = start + jnp.arange(Q_BLOCK)
        logits = jnp.einsum('bqhcd,bkhcd->bhcqk', q_blk, k).astype(jnp.float32) * A_SCALE
        logits = logits + rel_bias(table, q_pos, k_pos)[None, :, None]
        logits = jnp.where(k_pos[None, :] <= q_pos[:, None], logits, -jnp.inf)
        w = diff_weights(logits, lam).astype(v.dtype)
        return jnp.einsum('bhqk,bkhe->bqhe', w, v)

    out = lax.map(block, (qb, jnp.arange(nb) * Q_BLOCK))
    return jnp.moveaxis(out, 0, 1).reshape(B, S, A_HEADS, A_VDIM)


def attn_sample(q, k, v, lam, k_past, v_past, table):
    T = q.shape[1]
    P = k_past.shape[1]
    q_pos = P + jnp.arange(T)
    lp = jnp.einsum('bqhcd,bkhcd->bhcqk', q, k_past).astype(jnp.float32) * A_SCALE
    lp = lp + rel_bias(table, q_pos, jnp.arange(P))[None, :, None]
    ln = jnp.einsum('bqhcd,bkhcd->bhcqk', q, k).astype(jnp.float32) * A_SCALE
    ln = ln + rel_bias(table, q_pos, q_pos)[None, :, None]
    causal = jnp.arange(T)[None, :] <= jnp.arange(T)[:, None]
    ln = jnp.where(causal, ln, -jnp.inf)
    w = diff_weights(jnp.concatenate([lp, ln], axis=-1), lam).astype(v.dtype)
    return (jnp.einsum('bhqk,bkhe->bqhe', w[..., :P], v_past)
            + jnp.einsum('bhqk,bkhe->bqhe', w[..., P:], v))


def hgrn2_recurrence(q, k, v, log_f, s0):
    B, T, H, DK = q.shape
    DV = v.shape[-1]
    C = math.gcd(T, HGRN_CHUNK)
    n = T // C

    def to_chunks(a):
        return a.astype(jnp.float32).reshape(B, n, C, H, a.shape[-1]).transpose(1, 0, 3, 2, 4)

    mask = jnp.tril(jnp.ones((C, C), dtype=bool))[..., None]

    def step(S, xs):
        qc, kc, vc, gc = xs
        b = jnp.cumsum(gc, axis=2)
        o_inter = jnp.einsum('bhtk,bhkv->bhtv', qc * jnp.exp(b), S)
        diff = b[:, :, :, None, :] - b[:, :, None, :, :]
        decay = jnp.exp(jnp.where(mask, diff, -jnp.inf))
        A = jnp.einsum('bhtk,bhsk,bhtsk->bhts', qc, kc, decay)
        o = o_inter + jnp.einsum('bhts,bhsv->bhtv', A, vc)
        b_last = b[:, :, -1:, :]
        S_new = (jnp.exp(b_last[:, :, 0, :])[..., None] * S
                 + jnp.einsum('bhsk,bhsv->bhkv', kc * jnp.exp(b_last - b), vc))
        return S_new, o

    S_fin, o = lax.scan(step, s0.astype(jnp.float32),
                        (to_chunks(q), to_chunks(k), to_chunks(v), to_chunks(log_f)))
    o = o.transpose(1, 0, 3, 2, 4).reshape(B, T, H, DV)
    return o, S_fin


def rglru_branch(xc, buf, h0, conv_w, conv_b, wa, ba, wx, bx, lam_p, start_pos):
    B, T, W = xc.shape
    xp = jnp.concatenate([buf.astype(xc.dtype), xc], axis=1)
    u = conv_b + sum(xp[:, j:j + T] * conv_w[j] for j in range(CONV_W))
    new_buf = xp[:, -(CONV_W - 1):]
    ub = u.reshape(B, T, C_BLOCKS, C_BLOCK)
    r = jax.nn.sigmoid((jnp.einsum('btnc,ncd->btnd', ub, wa).reshape(B, T, W) + ba).astype(jnp.float32))
    i = jax.nn.sigmoid((jnp.einsum('btnc,ncd->btnd', ub, wx).reshape(B, T, W) + bx).astype(jnp.float32))
    log_a = -RG_C * r * jax.nn.softplus(-lam_p.astype(jnp.float32))
    a = jnp.exp(log_a)
    mult = jnp.sqrt(-jnp.expm1(2.0 * log_a))
    pos = start_pos + jnp.arange(T)
    mult = jnp.where((pos == 0)[None, :, None], 1.0, mult)
    b = mult * (i * u.astype(jnp.float32))
    b = b.at[:, 0].add(a[:, 0] * h0.astype(jnp.float32))

    def combine(left, right):
        return left[0] * right[0], right[0] * left[1] + right[1]

    _, h = lax.associative_scan(combine, (a, b), axis=1)
    return h, h[:, -1], new_buf


def mixer_sublayer(x, l, W, attend, s0, h0, buf, start_pos):
    B, T, _ = x.shape
    dt = x.dtype
    h = rmsnorm(x, W['g_mix'][l])
    proj = h @ W['w_in'][l]
    qa, ka, va, qb, fb, ib, gb, xc, gc = jnp.split(proj, SPLITS, axis=-1)
    q = rmsnorm(qa.reshape(B, T, A_HEADS, 2, A_SUB), W['q_norm'][l])
    k = rmsnorm(ka.reshape(B, T, A_HEADS, 2, A_SUB), W['k_norm'][l])
    v = va.reshape(B, T, A_HEADS, A_VDIM)
    lam_init = 0.8 - 0.6 * math.exp(-0.3 * l)
    f32 = jnp.float32
    lam = (jnp.exp(jnp.sum(W['lam_q1'][l].astype(f32) * W['lam_k1'][l].astype(f32)))
           - jnp.exp(jnp.sum(W['lam_q2'][l].astype(f32) * W['lam_k2'][l].astype(f32))) + lam_init)
    o_a = attend(q, k, v, lam)
    o_a = (rmsnorm(o_a, W['subln'][l]) * (1.0 - lam_init)).reshape(B, T, A_WIDTH)
    sm = jax.nn.softmax(W['hgrn_lb'].astype(f32), axis=0)
    lb = (jnp.cumsum(sm, axis=0) - sm[0:1])[l].reshape(B_HEADS, B_DK)
    f = lb + (1.0 - lb) * jax.nn.sigmoid(fb.astype(f32).reshape(B, T, B_HEADS, B_DK))
    kk = 1.0 - f
    qq = jax.nn.silu(qb.reshape(B, T, B_HEADS, B_DK))
    o_b, s_fin = hgrn2_recurrence(qq, kk, ib.reshape(B, T, B_HEADS, B_DV), jnp.log(f), s0)
    o_b = rmsnorm(o_b.astype(dt), W['hgrn_onorm'][l]) * jax.nn.silu(gb.reshape(B, T, B_HEADS, B_DV))
    o_b = o_b.reshape(B, T, B_WIDTH)
    hc, h_fin, new_buf = rglru_branch(xc, buf, h0, W['conv_w'][l], W['conv_b'][l], W['rg_wa'][l],
                                      W['rg_ba'][l], W['rg_wx'][l], W['rg_bx'][l], W['rg_lambda'][l],
                                      start_pos)
    o_c = hc.astype(dt) * jax.nn.gelu(gc)
    y = jnp.concatenate([o_a, o_b, o_c], axis=-1) @ W['w_out'][l]
    k_rows = k.reshape(B, T, A_HEADS, 2 * A_SUB)
    return x + y, k_rows, v, s_fin, h_fin, new_buf


def swiglu(h, wg, wu, wd):
    return (jax.nn.silu(h @ wg) * (h @ wu)) @ wd


def ffn_sublayer(x, l, W):
    B, T, D = x.shape
    h = rmsnorm(x, W['g_ffn'][l])
    m = l // 2
    if l % 2 == 0:
        y = swiglu(h, W['ff_gate'][m], W['ff_up'][m], W['ff_down'][m])
    else:
        hf = h.reshape(B * T, D)
        logits = (hf @ W['router'][m]).astype(jnp.float32)
        top_v, top_i = lax.top_k(logits, TOP_K)
        gates = jax.nn.softmax(top_v, axis=-1)
        comb = jnp.sum(jax.nn.one_hot(top_i, N_EXPERTS, dtype=jnp.float32) * gates[..., None], axis=1)
        yf = jnp.zeros_like(hf)
        for e in range(N_EXPERTS):
            ye = swiglu(hf, W['ex_gate'][m, e], W['ex_up'][m, e], W['ex_down'][m, e])
            yf = yf + comb[:, e:e + 1].astype(hf.dtype) * ye
        y = yf.reshape(B, T, D)
    return x + y


def setup_inputs(seed: int = 0) -> dict:
    key = jax.random.key(seed)
    ks = jax.random.split(key, 40)
    n_pages = PAST_LEN // PAGE_SIZE
    n_used = DEC_BATCH * n_pages
    n_pool = (n_used * 5) // 4
    nrm = jax.random.normal
    f32 = jnp.float32
    page_table = jax.random.permutation(ks[0], n_pool)[:n_used].reshape(DEC_BATCH, n_pages).astype(jnp.int32)
    a0 = jax.random.uniform(ks[1], (DEPTH, C_WIDTH), f32, 0.9, 0.999)
    s = a0 ** (1.0 / RG_C)
    rg_lambda = jnp.log(s) - jnp.log1p(-s)
    return {
        "x_prompt": nrm(ks[2], (BATCH, SEQ, D_MODEL), f32),
        "x_sample": nrm(ks[3], (DEC_BATCH, DEC_SEQ, D_MODEL), f32),
        "cache_k": nrm(ks[4], (DEPTH, n_pool, PAGE_SIZE, A_HEADS, 2 * A_SUB), f32),
        "cache_v": nrm(ks[5], (DEPTH, n_pool, PAGE_SIZE, A_HEADS, A_VDIM), f32),
        "page_table": page_table,
        "state_hgrn": 0.5 * nrm(ks[6], (DEPTH, DEC_BATCH, B_HEADS, B_DK, B_DV), f32),
        "state_rglru": 0.5 * nrm(ks[7], (DEPTH, DEC_BATCH, C_WIDTH), f32),
        "state_conv": nrm(ks[8], (DEPTH, DEC_BATCH, CONV_W - 1, C_WIDTH), f32),
        "w_in": nrm(ks[9], (DEPTH, D_MODEL, IN_COLS), f32) * D_MODEL ** -0.5,
        "w_out": nrm(ks[10], (DEPTH, D_MODEL, D_MODEL), f32) * D_MODEL ** -0.5,
        "g_mix": 1.0 + 0.02 * nrm(ks[11], (DEPTH, D_MODEL), f32),
        "g_ffn": 1.0 + 0.02 * nrm(ks[12], (DEPTH, D_MODEL), f32),
        "q_norm": 1.0 + 0.02 * nrm(ks[13], (DEPTH, A_SUB), f32),
        "k_norm": 1.0 + 0.02 * nrm(ks[14], (DEPTH, A_SUB), f32),
        "lam_q1": 0.1 * nrm(ks[15], (DEPTH, A_SUB), f32),
        "lam_k1": 0.1 * nrm(ks[16], (DEPTH, A_SUB), f32),
        "lam_q2": 0.1 * nrm(ks[17], (DEPTH, A_SUB), f32),
        "lam_k2": 0.1 * nrm(ks[18], (DEPTH, A_SUB), f32),
        "subln": 1.0 + 0.02 * nrm(ks[19], (DEPTH, A_VDIM), f32),
        "rel_table": 0.5 * nrm(ks[20], (N_BUCKETS, A_HEADS), f32),
        "hgrn_lb": 0.5 * nrm(ks[21], (DEPTH, B_WIDTH), f32),
        "hgrn_onorm": 1.0 + 0.02 * nrm(ks[22], (DEPTH, B_DV), f32),
        "conv_w": nrm(ks[23], (DEPTH, CONV_W, C_WIDTH), f32) * CONV_W ** -0.5,
        "conv_b": 0.01 * nrm(ks[24], (DEPTH, C_WIDTH), f32),
        "rg_wa": nrm(ks[25], (DEPTH, C_BLOCKS, C_BLOCK, C_BLOCK), f32) * C_BLOCK ** -0.5,
        "rg_ba": 0.01 * nrm(ks[26], (DEPTH, C_WIDTH), f32),
        "rg_wx": nrm(ks[27], (DEPTH, C_BLOCKS, C_BLOCK, C_BLOCK), f32) * C_BLOCK ** -0.5,
        "rg_bx": 0.01 * nrm(ks[28], (DEPTH, C_WIDTH), f32),
        "rg_lambda": rg_lambda,
        "ff_gate": nrm(ks[29], (N_DENSE, D_MODEL, FF_DIM), f32) * D_MODEL ** -0.5,
        "ff_up": nrm(ks[30], (N_DENSE, D_MODEL, FF_DIM), f32) * D_MODEL ** -0.5,
        "ff_down": nrm(ks[31], (N_DENSE, FF_DIM, D_MODEL), f32) * FF_DIM ** -0.5,
        "router": nrm(ks[32], (N_MOE, D_MODEL, N_EXPERTS), f32) * D_MODEL ** -0.5,
        "ex_gate": nrm(ks[33], (N_MOE, N_EXPERTS, D_MODEL, FF_DIM), f32) * D_MODEL ** -0.5,
        "ex_up": nrm(ks[34], (N_MOE, N_EXPERTS, D_MODEL, FF_DIM), f32) * D_MODEL ** -0.5,
        "ex_down": nrm(ks[35], (N_MOE, N_EXPERTS, FF_DIM, D_MODEL), f32) * FF_DIM ** -0.5,
    }


def reference(x_prompt, x_sample, cache_k, cache_v, page_table, state_hgrn, state_rglru, state_conv,
              w_in, w_out, g_mix, g_ffn, q_norm, k_norm, lam_q1, lam_k1, lam_q2, lam_k2, subln,
              rel_table, hgrn_lb, hgrn_onorm, conv_w, conv_b, rg_wa, rg_ba, rg_wx, rg_bx, rg_lambda,
              ff_gate, ff_up, ff_down, router, ex_gate, ex_up, ex_down):
    W = dict(w_in=w_in, w_out=w_out, g_mix=g_mix, g_ffn=g_ffn, q_norm=q_norm, k_norm=k_norm,
             lam_q1=lam_q1, lam_k1=lam_k1, lam_q2=lam_q2, lam_k2=lam_k2, subln=subln,
             hgrn_lb=hgrn_lb, hgrn_onorm=hgrn_onorm, conv_w=conv_w, conv_b=conv_b, rg_wa=rg_wa,
             rg_ba=rg_ba, rg_wx=rg_wx, rg_bx=rg_bx, rg_lambda=rg_lambda, ff_gate=ff_gate,
             ff_up=ff_up, ff_down=ff_down, router=router, ex_gate=ex_gate, ex_up=ex_up,
             ex_down=ex_down)
    B = x_prompt.shape[0]
    DB = x_sample.shape[0]
    past_len = page_table.shape[1] * PAGE_SIZE
    s0_p = jnp.zeros((B, B_HEADS, B_DK, B_DV), jnp.float32)
    h0_p = jnp.zeros((B, C_WIDTH), jnp.float32)
    buf_p = jnp.zeros((B, CONV_W - 1, C_WIDTH), x_prompt.dtype)
    attend_p = functools.partial(attn_prompt, table=rel_table)
    xp, xs = x_prompt, x_sample
    kp_l, vp_l, sp_l, hp_l, bp_l = [], [], [], [], []
    ks_l, vs_l, ss_l, hs_l, bs_l = [], [], [], [], []
    for l in range(DEPTH):
        k_past = cache_k[l, page_table].reshape(DB, past_len, A_HEADS, 2, A_SUB)
        v_past = cache_v[l, page_table].reshape(DB, past_len, A_HEADS, A_VDIM)
        attend_s = functools.partial(attn_sample, k_past=k_past, v_past=v_past, table=rel_table)
        xp, kp, vp, sp, hp, bp = mixer_sublayer(xp, l, W, attend_p, s0_p, h0_p, buf_p, 0)
        xs, kss, vss, sss, hss, bss = mixer_sublayer(xs, l, W, attend_s, state_hgrn[l],
                                                   state_rglru[l], state_conv[l], past_len)
        xp = ffn_sublayer(xp, l, W)
        xs = ffn_sublayer(xs, l, W)
        kp_l.append(kp); vp_l.append(vp); sp_l.append(sp); hp_l.append(hp); bp_l.append(bp)
        ks_l.append(kss); vs_l.append(vss); ss_l.append(sss); hs_l.append(hss); bs_l.append(bss)
    k_prompt = jnp.stack(kp_l)
    v_prompt = jnp.stack(vp_l)
    hgrn_prompt = jnp.stack(sp_l)
    rglru_prompt = jnp.stack(hp_l)
    conv_prompt = jnp.stack(bp_l)
    k_sample = jnp.stack(ks_l)
    v_sample = jnp.stack(vs_l)
    hgrn_sample = jnp.stack(ss_l)
    rglru_sample = jnp.stack(hs_l)
    conv_sample = jnp.stack(bs_l)
    return (xp, xs, k_prompt, v_prompt, hgrn_prompt, rglru_prompt, conv_prompt,
            k_sample, v_sample, hgrn_sample, rglru_sample, conv_sample)
```

```python
import functools
import math

import jax
import jax.numpy as jnp
from jax import lax
from jax.experimental import pallas as pl
from jax.experimental.pallas import tpu as pltpu

F32 = jnp.float32
BF16 = jnp.bfloat16
HI = lax.Precision.HIGHEST
EPS = 1e-6
NEG = -1e30
LANE = 128
SUBLANE = 8
NT = (((1,), (1,)), ((), ()))
TN = (((0,), (0,)), ((), ()))

A_HEADS = 8
A_SUB = 64
A_VDIM = 128
A_SCALE = A_SUB ** -0.5
B_HEADS = 4
C_BLOCKS = 4
CONV_W = 4
RG_C = 8.0
PAGE = 128
N_BUCKETS = 32
MAX_EXACT = 16
MAX_DISTANCE = 128
N_EXPERTS = 8
TOP_K = 2


def _cparams(sem, vmem_mb=48):
    return pltpu.CompilerParams(dimension_semantics=sem, vmem_limit_bytes=vmem_mb << 20)


def _pick(n, prefs):
    for p in prefs:
        if n % p == 0:
            return p
    raise ValueError(f"no tile in {prefs} divides {n}")


def _rms(x, g):
    return x * lax.rsqrt(jnp.mean(x * x, axis=-1, keepdims=True) + EPS) * g


def _norm_matmul_kernel(x_ref, g_ref, w_ref, o_ref, h_ref):
    @pl.when(pl.program_id(1) == 0)
    def _():
        h_ref[...] = _rms(x_ref[...], g_ref[...]).astype(BF16)

    o_ref[...] = jnp.dot(h_ref[...], w_ref[...], preferred_element_type=F32)


def norm_matmul(x, g, w):
    n, d = x.shape
    c = w.shape[1]
    tm = _pick(n, (1024, 512, 256, 128))
    tn = _pick(c, (512, 256, 128))
    return pl.pallas_call(
        _norm_matmul_kernel,
        grid=(n // tm, c // tn),
        in_specs=[pl.BlockSpec((tm, d), lambda i, j: (i, 0)),
                  pl.BlockSpec((1, d), lambda i, j: (0, 0)),
                  pl.BlockSpec((d, tn), lambda i, j: (0, j))],
        out_specs=pl.BlockSpec((tm, tn), lambda i, j: (i, j)),
        out_shape=jax.ShapeDtypeStruct((n, c), F32),
        scratch_shapes=[pltpu.VMEM((tm, d), BF16)],
        compiler_params=_cparams(("parallel", "arbitrary")),
        name="norm_matmul",
    )(x, g.reshape(1, d), w)


def _qk_prep_kernel(q_ref, k_ref, v_ref, qg_ref, kg_ref, qf_ref, kf_ref, qb_ref, kb_ref, vb_ref):
    r = lax.broadcasted_iota(jnp.int32, (LANE, LANE), 0)
    c = lax.broadcasted_iota(jnp.int32, (LANE, LANE), 1)
    seg = jnp.where((r // A_SUB) == (c // A_SUB), 1.0 / A_SUB, 0.0).astype(F32)
    for h in range(A_HEADS):
        sl = slice(h * LANE, (h + 1) * LANE)
        for src, g_ref, of, ob, scale in ((q_ref, qg_ref, qf_ref, qb_ref, A_SCALE),
                                          (k_ref, kg_ref, kf_ref, kb_ref, None)):
            x = src[:, sl]
            ms = jnp.dot(x * x, seg, precision=HI, preferred_element_type=F32)
            y = x * lax.rsqrt(ms + EPS) * g_ref[...]
            if scale is not None:
                y = y * scale
            of[:, sl] = y
            ob[:, sl] = y.astype(BF16)
    vb_ref[...] = v_ref[...].astype(BF16)


def qk_prep(proj, q_norm, k_norm):
    n = proj.shape[0]
    w = A_HEADS * LANE
    tm = _pick(n, (512, 256, 128))
    qg = jnp.concatenate([q_norm, q_norm]).reshape(1, LANE)
    kg = jnp.concatenate([k_norm, k_norm]).reshape(1, LANE)
    col = lambda cb: pl.BlockSpec((tm, w), lambda i, cb=cb: (i, cb))
    vec = pl.BlockSpec((1, LANE), lambda i: (0, 0))
    out = pl.BlockSpec((tm, w), lambda i: (i, 0))
    return pl.pallas_call(
        _qk_prep_kernel,
        grid=(n // tm,),
        in_specs=[col(0), col(1), col(2), vec, vec],
        out_specs=[out] * 5,
        out_shape=[jax.ShapeDtypeStruct((n, w), F32)] * 2 + [jax.ShapeDtypeStruct((n, w), BF16)] * 3,
        compiler_params=_cparams(("parallel",)),
        name="qk_prep",
    )(proj, proj, proj, qg, kg)


def _rel_bias_vals(table, n):
    n = jnp.maximum(n, 0)
    nf = jnp.maximum(n, 1).astype(F32)
    far = MAX_EXACT + (jnp.log(nf / MAX_EXACT) / math.log(MAX_DISTANCE / MAX_EXACT)
                       * (N_BUCKETS - MAX_EXACT)).astype(jnp.int32)
    bucket = jnp.where(n < MAX_EXACT, n, jnp.minimum(far, N_BUCKETS - 1))
    return table[bucket].astype(F32)


def _lam_from(lamp, lam_init):
    s1 = jnp.sum(lamp[0:1] * lamp[1:2], axis=-1, keepdims=True)
    s2 = jnp.sum(lamp[2:3] * lamp[3:4], axis=-1, keepdims=True)
    return jnp.exp(s1) - jnp.exp(s2) + lam_init


def _attn_prompt_kernel(q_ref, k_ref, v_ref, bias_ref, lamp_ref, subln_ref, o_ref,
                        m_ref, l_ref, acc_ref, *, T, lam_init):
    qi = pl.program_id(2)
    q = q_ref[...]
    lane = lax.broadcasted_iota(jnp.int32, q.shape, 1)
    zero = jnp.zeros_like(q)
    qq = jnp.concatenate([jnp.where(lane < A_SUB, q, zero), jnp.where(lane >= A_SUB, q, zero)], axis=0)
    m_ref[...] = jnp.full(m_ref.shape, NEG, F32)
    l_ref[...] = jnp.zeros(l_ref.shape, F32)
    acc_ref[...] = jnp.zeros(acc_ref.shape, F32)

    def tile(kt, bias):
        start = pl.multiple_of(kt * T, T)
        k = k_ref[pl.ds(start, T), :]
        v = v_ref[pl.ds(start, T), :]
        s = lax.dot_general(qq, k, NT, preferred_element_type=F32)
        if bias is not None:
            s = s + jnp.concatenate([bias, bias], axis=0)
        m_old = m_ref[...]
        m_new = jnp.maximum(m_old, jnp.max(s, axis=-1, keepdims=True))
        alpha = jnp.exp(m_old - m_new)
        p = jnp.exp(s - m_new)
        l_ref[...] = alpha * l_ref[...] + jnp.sum(p, axis=-1, keepdims=True)
        acc_ref[...] = alpha * acc_ref[...] + jnp.dot(p.astype(BF16), v, preferred_element_type=F32)
        m_ref[...] = m_new

    def far_body(kt, carry):
        tile(kt, None)
        return carry

    lax.fori_loop(0, jnp.maximum(qi - 1, 0), far_body, 0)

    @pl.when(qi >= 1)
    def _():
        tile(qi - 1, bias_ref[1])

    tile(qi, bias_ref[0])

    lam = _lam_from(lamp_ref[...], lam_init)
    o = acc_ref[...] / l_ref[...]
    o = o[:T] - lam * o[T:]
    o_ref[...] = (_rms(o, subln_ref[...]) * (1.0 - lam_init)).astype(BF16)


def attn_prompt(qb, kb, vb, rel_table, lamp, subln, B, S, lam_init):
    T = _pick(S, (256, 128))
    nq = S // T
    i = jnp.arange(T)
    d0 = i[:, None] - i[None, :]
    far = rel_table[N_BUCKETS - 1].astype(F32)
    b0 = jnp.where((d0 >= 0)[..., None], _rel_bias_vals(rel_table, d0) - far, NEG)
    b1 = _rel_bias_vals(rel_table, d0 + T) - far
    bias = jnp.moveaxis(jnp.stack([b0, b1]), -1, 0)
    return pl.pallas_call(
        functools.partial(_attn_prompt_kernel, T=T, lam_init=lam_init),
        grid=(B, A_HEADS, nq),
        in_specs=[pl.BlockSpec((T, LANE), lambda b, h, i: (b * nq + i, h)),
                  pl.BlockSpec((S, LANE), lambda b, h, i: (b, h)),
                  pl.BlockSpec((S, LANE), lambda b, h, i: (b, h)),
                  pl.BlockSpec((None, 2, T, T), lambda b, h, i: (h, 0, 0, 0)),
                  pl.BlockSpec((4, A_SUB), lambda b, h, i: (0, 0)),
                  pl.BlockSpec((1, LANE), lambda b, h, i: (0, 0))],
        out_specs=pl.BlockSpec((T, LANE), lambda b, h, i: (b * nq + i, h)),
        out_shape=jax.ShapeDtypeStruct((B * S, A_HEADS * LANE), BF16),
        scratch_shapes=[pltpu.VMEM((2 * T, 1), F32), pltpu.VMEM((2 * T, 1), F32),
                        pltpu.VMEM((2 * T, LANE), F32)],
        compiler_params=_cparams(("parallel", "parallel", "arbitrary")),
        name="attn_prompt",
    )(qb, kb, vb, bias, lamp, subln.reshape(1, LANE))


def _attn_sample_kernel(pt_ref, q_ref, kn_ref, vn_ref, mask_ref, bnew_ref, lamp_ref, subln_ref, *rest,
                        PPS, n_steps, lam_init):
    k_refs = rest[:PPS]
    v_refs = rest[PPS:2 * PPS]
    o_ref = rest[2 * PPS]
    qm_ref, m_ref, l_ref, acc_ref = rest[2 * PPS + 1:]
    step = pl.program_id(1)
    Tq = q_ref.shape[0]

    @pl.when(step == 0)
    def _():
        q = q_ref[...]
        lane = lax.broadcasted_iota(jnp.int32, (Tq, LANE), 1)
        rows = []
        for h in range(A_HEADS):
            qh = q[:, h * LANE:(h + 1) * LANE]
            rows.append(jnp.where(lane < A_SUB, qh, 0.0))
            rows.append(jnp.where(lane >= A_SUB, qh, 0.0))
        qm_ref[...] = jnp.concatenate(rows, axis=0).astype(BF16)
        m_ref[...] = jnp.full(m_ref.shape, NEG, F32)
        l_ref[...] = jnp.zeros(l_ref.shape, F32)
        acc_ref[...] = jnp.zeros(acc_ref.shape, F32)

    def update(s, v2):
        m_old = m_ref[...]
        m_new = jnp.maximum(m_old, jnp.max(s, axis=-1, keepdims=True))
        alpha = jnp.exp(m_old - m_new)
        p = jnp.exp(s - m_new)
        l_ref[...] = alpha * l_ref[...] + jnp.sum(p, axis=-1, keepdims=True)
        acc_ref[...] = alpha * acc_ref[...] + jnp.dot(p.astype(BF16), v2, preferred_element_type=F32)
        m_ref[...] = m_new

    qm = qm_ref[...]
    for j in range(PPS):
        k2 = k_refs[j][...].reshape(PAGE * A_HEADS, LANE).astype(BF16)
        v2 = v_refs[j][...].reshape(PAGE * A_HEADS, LANE).astype(BF16)
        s = lax.dot_general(qm, k2, NT, preferred_element_type=F32)
        if j == PPS - 1:
            msk = mask_ref[jnp.where(step == n_steps - 1, 1, 0)]
        else:
            msk = mask_ref[0]
        update(s + msk, v2)

    @pl.when(step == n_steps - 1)
    def _():
        kn2 = kn_ref[...].astype(BF16)
        vn2 = vn_ref[...].astype(BF16)
        s = lax.dot_general(qm, kn2, NT, preferred_element_type=F32)
        update(s + bnew_ref[...], vn2)
        lam = _lam_from(lamp_ref[...], lam_init)
        o = acc_ref[...] / l_ref[...]
        for h in range(A_HEADS):
            r0 = h * 2 * Tq
            oh = o[r0:r0 + Tq] - lam * o[r0 + Tq:r0 + 2 * Tq]
            o_ref[:, h * LANE:(h + 1) * LANE] = _rms(oh, subln_ref[...]) * (1.0 - lam_init)


def attn_sample(qf, kn2, vn2, cache_k, cache_v, layer, page_table, rel_table, lamp, subln, NP, DB, Tq, lam_init):
    n_pages = page_table.shape[1]
    PPS = _pick(n_pages, (8, 4, 2, 1))
    n_steps = n_pages // PPS
    P = n_pages * PAGE
    R = A_HEADS * 2 * Tq
    L = PAGE * A_HEADS
    far = rel_table[N_BUCKETS - 1].astype(F32)
    rr = jnp.arange(R)
    rh, rt = rr // (2 * Tq), rr % Tq
    ll = jnp.arange(L)
    lk, lh = ll // A_HEADS, ll % A_HEADS
    same = rh[:, None] == lh[None, :]
    mask_far = jnp.where(same, 0.0, NEG).astype(F32)
    n_last = (P + rt[:, None]) - (P - PAGE + lk[None, :])
    bl = _rel_bias_vals(rel_table, n_last)
    bl = jnp.take_along_axis(bl, jnp.broadcast_to(rh[:, None, None], (R, L, 1)), axis=2)[..., 0] - far[rh][:, None]
    mask_last = jnp.where(same, bl, NEG).astype(F32)
    masks = jnp.stack([mask_far, mask_last])
    ln = jnp.arange(Tq * A_HEADS)
    nk, nh = ln // A_HEADS, ln % A_HEADS
    n_new = rt[:, None] - nk[None, :]
    bn = _rel_bias_vals(rel_table, n_new)
    bn = jnp.take_along_axis(bn, jnp.broadcast_to(rh[:, None, None], (R, Tq * A_HEADS, 1)), axis=2)[..., 0] - far[rh][:, None]
    bnew = jnp.where((rh[:, None] == nh[None, :]) & (n_new >= 0), bn, NEG).astype(F32)

    pt_flat = page_table.reshape(-1).astype(jnp.int32)
    page_spec = lambda j: pl.BlockSpec(
        (None, None, PAGE, A_HEADS, LANE),
        lambda b, s, pt, j=j: (layer, pt[b * n_pages + s * PPS + j], 0, 0, 0))
    const2 = lambda shape: pl.BlockSpec(shape, lambda b, s, pt: (0,) * len(shape))
    in_specs = ([pl.BlockSpec((Tq, A_HEADS * LANE), lambda b, s, pt: (NP // Tq + b, 0)),
                 pl.BlockSpec((None, Tq * A_HEADS, LANE), lambda b, s, pt: (b, 0, 0)),
                 pl.BlockSpec((None, Tq * A_HEADS, LANE), lambda b, s, pt: (b, 0, 0)),
                 const2((2, R, L)), const2((R, Tq * A_HEADS)), const2((4, A_SUB)), const2((1, LANE))]
                + [page_spec(j) for j in range(PPS)] * 2)
    grid_spec = pltpu.PrefetchScalarGridSpec(
        num_scalar_prefetch=1,
        grid=(DB, n_steps),
        in_specs=in_specs,
        out_specs=pl.BlockSpec((Tq, A_HEADS * LANE), lambda b, s, pt: (b, 0)),
        scratch_shapes=[pltpu.VMEM((R, LANE), BF16), pltpu.VMEM((R, 1), F32), pltpu.VMEM((R, 1), F32),
                        pltpu.VMEM((R, LANE), F32)])
    return pl.pallas_call(
        functools.partial(_attn_sample_kernel, PPS=PPS, n_steps=n_steps, lam_init=lam_init),
        grid_spec=grid_spec,
        out_shape=jax.ShapeDtypeStruct((DB * Tq, A_HEADS * LANE), F32),
        compiler_params=_cparams(("parallel", "arbitrary")),
        name="attn_sample",
    )(pt_flat, qf, kn2, vn2, masks, bnew, lamp, subln.reshape(1, LANE),
      *([cache_k] * PPS), *([cache_v] * PPS))


HG_C = 128


def _hgrn_kernel(*refs, R, levels, has_state):
    if has_state:
        q_ref, f_ref, i_ref, g_ref, lb_ref, on_ref, s0_ref, o_ref, sf_ref, S_ref = refs
    else:
        q_ref, f_ref, i_ref, g_ref, lb_ref, on_ref, o_ref, sf_ref, S_ref = refs
    c = pl.program_id(2)

    @pl.when(c == 0)
    def _():
        S_ref[...] = s0_ref[...] if has_state else jnp.zeros(S_ref.shape, F32)

    def rows(ref):
        x = ref[...]
        if R < HG_C:
            x = jnp.concatenate([x, jnp.zeros((HG_C - R, LANE), F32)], axis=0)
        return x

    row = lax.broadcasted_iota(jnp.int32, (HG_C, HG_C), 0)
    col = lax.broadcasted_iota(jnp.int32, (HG_C, HG_C), 1)
    rvec = lax.broadcasted_iota(jnp.int32, (HG_C, LANE), 0)
    lb = lb_ref[...]
    f = lb + (1.0 - lb) * jax.nn.sigmoid(rows(f_ref))
    g = jnp.log(f)
    kk = 1.0 - f
    qq = jax.nn.silu(rows(q_ref))
    v = rows(i_ref)
    if R < HG_C:
        live = rvec < R
        g = jnp.where(live, g, 0.0)
        kk = jnp.where(live, kk, 0.0)
        qq = jnp.where(live, qq, 0.0)
    dot = functools.partial(jnp.dot, precision=HI, preferred_element_type=F32)
    b = dot(jnp.where(row >= col, 1.0, 0.0).astype(F32), g)
    A = jnp.where(row == col, jnp.sum(qq * kk, axis=-1, keepdims=True), 0.0)
    for lv in range(levels):
        s = 1 << lv
        mid = (row // (2 * s)) * (2 * s) + (s - 1)
        r = dot(jnp.where(col == mid, 1.0, 0.0).astype(F32), b)
        second = (rvec % (2 * s)) >= s
        ql = qq * jnp.exp(jnp.where(second, b - r, -jnp.inf))
        kl = kk * jnp.exp(jnp.where(second, -jnp.inf, r - b))
        al = lax.dot_general(ql, kl, NT, precision=HI, preferred_element_type=F32)
        A = A + jnp.where((row // (2 * s)) == (col // (2 * s)), al, 0.0)
    S = S_ref[...]
    o = dot(qq * jnp.exp(b), S) + dot(A, v)
    b_last = b[HG_C - 1:HG_C]
    e_last = jnp.exp(b_last)
    kd = kk * jnp.exp(b_last - b)
    S_new = dot(jnp.where(row == col, e_last, 0.0), S) + lax.dot_general(kd, v, TN, precision=HI,
                                                                        preferred_element_type=F32)
    S_ref[...] = S_new
    on = _rms(o, on_ref[...]) * jax.nn.silu(rows(g_ref))
    o_ref[...] = on[:R].astype(BF16)

    @pl.when(c == pl.num_programs(2) - 1)
    def _():
        sf_ref[...] = S_new


def hgrn(proj, lb, onorm, row0, nb, T, s0, layer):
    R = min(T, HG_C)
    nc = T // R
    levels = int(math.log2(R))
    base = row0 // R
    qcol, fcol, icol, gcol = (3 * A_HEADS + k * B_HEADS for k in range(4))
    blk = lambda cb: pl.BlockSpec((R, LANE), lambda b, h, c, cb=cb: (base + b * nc + c, cb + h))
    in_specs = [blk(qcol), blk(fcol), blk(icol), blk(gcol),
                pl.BlockSpec((None, 1, LANE), lambda b, h, c: (h, 0, 0)),
                pl.BlockSpec((1, LANE), lambda b, h, c: (0, 0))]
    args = [proj, proj, proj, proj, lb.reshape(B_HEADS, 1, LANE), onorm.reshape(1, LANE)]
    if s0 is not None:
        in_specs.append(pl.BlockSpec((None, None, None, LANE, LANE), lambda b, h, c: (layer, b, h, 0, 0)))
        args.append(s0)
    return pl.pallas_call(
        functools.partial(_hgrn_kernel, R=R, levels=levels, has_state=s0 is not None),
        grid=(nb, B_HEADS, nc),
        in_specs=in_specs,
        out_specs=[pl.BlockSpec((R, LANE), lambda b, h, c: (b * nc + c, h)),
                   pl.BlockSpec((None, None, LANE, LANE), lambda b, h, c: (b, h, 0, 0))],
        out_shape=[jax.ShapeDtypeStruct((nb * T, B_HEADS * LANE), BF16),
                   jax.ShapeDtypeStruct((nb, B_HEADS, LANE, LANE), F32)],
        scratch_shapes=[pltpu.VMEM((LANE, LANE), F32)],
        compiler_params=_cparams(("parallel", "parallel", "arbitrary")),
        name="hgrn",
    )(*args)


def _rg_gates(u, wa_ref, wx_ref, ba, bx, sp):
    ra, rx = [], []
    for n in range(C_BLOCKS):
        un = u[:, n * LANE:(n + 1) * LANE]
        ra.append(jnp.dot(un, wa_ref[n], precision=HI, preferred_element_type=F32))
        rx.append(jnp.dot(un, wx_ref[n], precision=HI, preferred_element_type=F32))
    r = jax.nn.sigmoid(jnp.concatenate(ra, axis=-1) + ba)
    ig = jax.nn.sigmoid(jnp.concatenate(rx, axis=-1) + bx)
    log_a = -RG_C * r * sp
    a = jnp.exp(log_a)
    z = 2.0 * log_a
    u2 = jnp.exp(z)
    em1 = jnp.where(u2 == 1.0, z, (u2 - 1.0) * z / jnp.log(u2))
    mult = jnp.sqrt(-em1)
    return a, mult, ig * u


def _scan_rows(a, b, t, n_steps):
    n = a.shape[0]
    for lv in range(n_steps):
        d = 1 << lv
        ok = t >= d
        a_s = jnp.where(ok, pltpu.roll(a, d, 0), 1.0)
        b_s = jnp.where(ok, pltpu.roll(b, d, 0), 0.0)
        b = a * b_s + b
        a = a * a_s
    return a, b


def _rglru_prompt_kernel(x_ref, gc_ref, cw_ref, cb_ref, wa_ref, wx_ref, ba_ref, bx_ref, lam_ref,
                         o_ref, hf_ref, prev_ref, h_ref, *, Tc):
    i = pl.program_id(1)

    @pl.when(i == 0)
    def _():
        prev_ref[...] = jnp.zeros(prev_ref.shape, F32)
        h_ref[...] = jnp.zeros(h_ref.shape, F32)

    x = x_ref[...]
    prev = prev_ref[...]
    t8 = lax.broadcasted_iota(jnp.int32, prev.shape, 0)
    cw = cw_ref[...]
    u = cb_ref[...] + x * cw[CONV_W - 1:CONV_W]
    for j in range(CONV_W - 1):
        d = CONV_W - 1 - j
        xs = pltpu.roll(x, d, 0)
        head = jnp.where(t8 >= d, xs[:SUBLANE], pltpu.roll(prev, d, 0))
        u = u + jnp.concatenate([head, xs[SUBLANE:]], axis=0) * cw[j:j + 1]
    prev_ref[...] = x[Tc - SUBLANE:]
    sp = jax.nn.softplus(-lam_ref[...])
    a, mult, iu = _rg_gates(u, wa_ref, wx_ref, ba_ref[...], bx_ref[...], sp)
    t = lax.broadcasted_iota(jnp.int32, a.shape, 0)
    mult = jnp.where((t == 0) & (i == 0), 1.0, mult)
    acc_a, acc_b = _scan_rows(a, mult * iu, t, int(math.log2(Tc)))
    h = acc_a * h_ref[0:1] + acc_b
    h_ref[...] = jnp.broadcast_to(h[Tc - 1:Tc], h_ref.shape)
    o_ref[...] = (h * jax.nn.gelu(gc_ref[...])).astype(BF16)

    @pl.when(i == pl.num_programs(1) - 1)
    def _():
        hf_ref[...] = h[Tc - 1:Tc]


def _rg_weight_specs(W, nidx):
    z = lambda shape: pl.BlockSpec(shape, lambda *_: (0,) * len(shape))
    return [z((CONV_W, W)), z((1, W)), z((C_BLOCKS, LANE, LANE)), z((C_BLOCKS, LANE, LANE)),
            z((1, W)), z((1, W)), z((1, W))]


def rglru_prompt(proj, wts, B, S):
    W = C_BLOCKS * LANE
    Tc = _pick(S, (512, 256, 128))
    nt = S // Tc
    xcol = (3 * A_HEADS + 4 * B_HEADS) * LANE // W
    return pl.pallas_call(
        functools.partial(_rglru_prompt_kernel, Tc=Tc),
        grid=(B, nt),
        in_specs=[pl.BlockSpec((Tc, W), lambda b, i: (b * nt + i, xcol)),
                  pl.BlockSpec((Tc, W), lambda b, i: (b * nt + i, xcol + 1))] + _rg_weight_specs(W, 2),
        out_specs=[pl.BlockSpec((Tc, W), lambda b, i: (b * nt + i, 0)),
                   pl.BlockSpec((None, 1, W), lambda b, i: (b, 0, 0))],
        out_shape=[jax.ShapeDtypeStruct((B * S, W), BF16), jax.ShapeDtypeStruct((B, 1, W), F32)],
        scratch_shapes=[pltpu.VMEM((SUBLANE, W), F32), pltpu.VMEM((SUBLANE, W), F32)],
        compiler_params=_cparams(("parallel", "arbitrary")),
        name="rglru_prompt",
    )(proj, proj, *wts)


def _rglru_sample_kernel(x_ref, gc_ref, p_ref, h0_ref, cw_ref, cb_ref, wa_ref, wx_ref, ba_ref, bx_ref, lam_ref,
                         o_ref, h_ref, *, Tq):
    x = x_ref[...]
    n = x.shape[0]
    t = lax.broadcasted_iota(jnp.int32, x.shape, 0) % Tq
    p = p_ref[...]
    cw = cw_ref[...]
    u = cb_ref[...] + x * cw[CONV_W - 1:CONV_W]
    for j in range(CONV_W - 1):
        d = CONV_W - 1 - j
        shifted = jnp.where(t >= d, pltpu.roll(x, d, 0), pltpu.roll(p, n - (Tq - d), 0))
        u = u + shifted * cw[j:j + 1]
    sp = jax.nn.softplus(-lam_ref[...])
    a, mult, iu = _rg_gates(u, wa_ref, wx_ref, ba_ref[...], bx_ref[...], sp)
    acc_a, acc_b = _scan_rows(a, mult * iu, t, int(math.log2(Tq)))
    h = acc_a * h0_ref[...] + acc_b
    h_ref[...] = h
    o_ref[...] = (h * jax.nn.gelu(gc_ref[...])).astype(BF16)


def rglru_sample(proj, wts, pbuf, h0rep, NP, NS, Tq):
    W = C_BLOCKS * LANE
    tr = _pick(NS, (256, 128, 64, 32, 16, 8))
    xcol = (3 * A_HEADS + 4 * B_HEADS) * LANE // W
    base = NP // tr
    loc = pl.BlockSpec((tr, W), lambda i: (i, 0))
    return pl.pallas_call(
        functools.partial(_rglru_sample_kernel, Tq=Tq),
        grid=(NS // tr,),
        in_specs=[pl.BlockSpec((tr, W), lambda i: (base + i, xcol)),
                  pl.BlockSpec((tr, W), lambda i: (base + i, xcol + 1)), loc, loc] + _rg_weight_specs(W, 1),
        out_specs=[loc, loc],
        out_shape=[jax.ShapeDtypeStruct((NS, W), BF16), jax.ShapeDtypeStruct((NS, W), F32)],
        compiler_params=_cparams(("parallel",)),
        name="rglru_sample",
    )(proj, proj, pbuf, h0rep, *wts)


def _out_proj_kernel(x_ref, a_ref, b_ref, c_ref, wa_ref, wb_ref, wc_ref, o_ref):
    y = jnp.dot(a_ref[...], wa_ref[...], preferred_element_type=F32)
    y = y + jnp.dot(b_ref[...], wb_ref[...], preferred_element_type=F32)
    y = y + jnp.dot(c_ref[...], wc_ref[...], preferred_element_type=F32)
    o_ref[...] = x_ref[...] + y


def out_proj(x, oa, ob, oc, w):
    n, d = x.shape
    wa_, wb_, wc_ = oa.shape[1], ob.shape[1], oc.shape[1]
    assert wb_ == wc_ and wa_ % wb_ == 0
    tm = _pick(n, (1024, 512, 256, 128))
    tn = _pick(d, (512, 256, 128))
    return pl.pallas_call(
        _out_proj_kernel,
        grid=(n // tm, d // tn),
        in_specs=[pl.BlockSpec((tm, tn), lambda i, j: (i, j)),
                  pl.BlockSpec((tm, wa_), lambda i, j: (i, 0)),
                  pl.BlockSpec((tm, wb_), lambda i, j: (i, 0)),
                  pl.BlockSpec((tm, wc_), lambda i, j: (i, 0)),
                  pl.BlockSpec((wa_, tn), lambda i, j: (0, j)),
                  pl.BlockSpec((wb_, tn), lambda i, j: (wa_ // wb_, j)),
                  pl.BlockSpec((wc_, tn), lambda i, j: (wa_ // wb_ + 1, j))],
        out_specs=pl.BlockSpec((tm, tn), lambda i, j: (i, j)),
        out_shape=jax.ShapeDtypeStruct((n, d), F32),
        compiler_params=_cparams(("parallel", "arbitrary")),
        name="out_proj",
    )(x, oa, ob, oc, w, w, w)


def _ffn_up_kernel(te_ref, nu_ref, x_ref, g_ref, wg_ref, wu_ref, o_ref, h_ref, *, do_norm):
    i = pl.program_id(0)

    @pl.when(pl.program_id(1) == 0)
    def _():
        x = x_ref[...]
        if do_norm:
            x = _rms(x, g_ref[...])
        h_ref[...] = x.astype(BF16)

    @pl.when(i < nu_ref[0])
    def _():
        h = h_ref[...]
        a = jnp.dot(h, wg_ref[...], preferred_element_type=F32)
        b = jnp.dot(h, wu_ref[...], preferred_element_type=F32)
        o_ref[...] = (jax.nn.silu(a) * b).astype(BF16)

    @pl.when(i >= nu_ref[0])
    def _():
        o_ref[...] = jnp.zeros(o_ref.shape, BF16)


def ffn_up(x, g, wg, wu, te, nu, tm, do_norm):
    n, d = x.shape
    f = wg.shape[-1]
    tf = _pick(f, (512, 256, 128))
    grid_spec = pltpu.PrefetchScalarGridSpec(
        num_scalar_prefetch=2,
        grid=(n // tm, f // tf),
        in_specs=[pl.BlockSpec((tm, d), lambda i, j, te, nu: (i, 0)),
                  pl.BlockSpec((1, d), lambda i, j, te, nu: (0, 0)),
                  pl.BlockSpec((None, d, tf), lambda i, j, te, nu: (te[i], 0, j)),
                  pl.BlockSpec((None, d, tf), lambda i, j, te, nu: (te[i], 0, j))],
        out_specs=pl.BlockSpec((tm, tf), lambda i, j, te, nu: (i, j)),
        scratch_shapes=[pltpu.VMEM((tm, d), BF16)])
    return pl.pallas_call(
        functools.partial(_ffn_up_kernel, do_norm=do_norm),
        grid_spec=grid_spec,
        out_shape=jax.ShapeDtypeStruct((n, f), BF16),
        compiler_params=_cparams(("parallel", "arbitrary")),
        name="ffn_up",
    )(te, nu, x, g.reshape(1, d), wg, wu)


def _ffn_down_kernel(te_ref, nu_ref, a_ref, wd_ref, *rest, residual):
    i = pl.program_id(0)
    if residual:
        r_ref, o_ref = rest
    else:
        (o_ref,) = rest

    @pl.when(i < nu_ref[0])
    def _():
        y = jnp.dot(a_ref[...], wd_ref[...], preferred_element_type=F32)
        if residual:
            y = r_ref[...] + y
        o_ref[...] = y

    @pl.when(i >= nu_ref[0])
    def _():
        o_ref[...] = jnp.zeros(o_ref.shape, F32)


def ffn_down(act, wd, te, nu, tm, res):
    n, f = act.shape
    d = wd.shape[-1]
    tn = _pick(d, (512, 256, 128))
    in_specs = [pl.BlockSpec((tm, f), lambda i, j, te, nu: (i, 0)),
                pl.BlockSpec((None, f, tn), lambda i, j, te, nu: (te[i], 0, j))]
    args = [act, wd]
    if res is not None:
        in_specs.append(pl.BlockSpec((tm, tn), lambda i, j, te, nu: (i, j)))
        args.append(res)
    grid_spec = pltpu.PrefetchScalarGridSpec(
        num_scalar_prefetch=2,
        grid=(n // tm, d // tn),
        in_specs=in_specs,
        out_specs=pl.BlockSpec((tm, tn), lambda i, j, te, nu: (i, j)))
    return pl.pallas_call(
        functools.partial(_ffn_down_kernel, residual=res is not None),
        grid_spec=grid_spec,
        out_shape=jax.ShapeDtypeStruct((n, d), F32),
        compiler_params=_cparams(("parallel", "arbitrary")),
        name="ffn_down",
    )(te, nu, *args)


def _router_kernel(x_ref, g_ref, wr_ref, h_ref, rt_ref):
    h = _rms(x_ref[...], g_ref[...])
    h_ref[...] = h
    logits = jnp.dot(h, wr_ref[...], precision=HI, preferred_element_type=F32)
    lane = lax.broadcasted_iota(jnp.int32, logits.shape, 1)
    l1 = jnp.where(lane < N_EXPERTS, logits, -jnp.inf)
    m1 = jnp.max(l1, axis=-1, keepdims=True)
    i1 = jnp.min(jnp.where(l1 == m1, lane, LANE), axis=-1, keepdims=True)
    l2 = jnp.where(lane == i1, -jnp.inf, l1)
    m2 = jnp.max(l2, axis=-1, keepdims=True)
    i2 = jnp.min(jnp.where(l2 == m2, lane, LANE), axis=-1, keepdims=True)
    e = jnp.exp(m2 - m1)
    g1 = 1.0 / (1.0 + e)
    g2 = e / (1.0 + e)
    rt_ref[...] = jnp.where(lane == 0, i1.astype(F32),
                            jnp.where(lane == 1, i2.astype(F32),
                                      jnp.where(lane == 2, g1, jnp.where(lane == 3, g2, 0.0))))


def route_tokens(x, g, wr):
    n, d = x.shape
    tm = _pick(n, (512, 256, 128))
    wr_pad = jnp.zeros((d, LANE), F32).at[:, :N_EXPERTS].set(wr)
    return pl.pallas_call(
        _router_kernel,
        grid=(n // tm,),
        in_specs=[pl.BlockSpec((tm, d), lambda i: (i, 0)),
                  pl.BlockSpec((1, d), lambda i: (0, 0)),
                  pl.BlockSpec((d, LANE), lambda i: (0, 0))],
        out_specs=[pl.BlockSpec((tm, d), lambda i: (i, 0)), pl.BlockSpec((tm, LANE), lambda i: (i, 0))],
        out_shape=[jax.ShapeDtypeStruct((n, d), F32), jax.ShapeDtypeStruct((n, LANE), F32)],
        compiler_params=_cparams(("parallel",)),
        name="router",
    )(x, g.reshape(1, d), wr_pad)


DISPATCH_CHUNK = 64
COMBINE_ROWS = 128


def _row_copy(src, i, dst, j, sem):
    return pltpu.make_async_copy(src.at[pl.ds(i, 1)], dst.at[pl.ds(j, 1)], sem)


def _dispatch_kernel(pos_ref, h_ref, z_ref, o_ref, sem, *, n):
    del z_ref
    n_chunks = n // DISPATCH_CHUNK

    def wait_chunk():
        def w(t, c):
            _row_copy(h_ref, 0, o_ref, 0, sem).wait()
            return c
        lax.fori_loop(0, TOP_K * DISPATCH_CHUNK, w, 0)

    def chunk(ci, carry):
        def issue(t, c):
            tok = ci * DISPATCH_CHUNK + t
            for k in range(TOP_K):
                _row_copy(h_ref, tok, o_ref, pos_ref[k * n + tok], sem).start()
            return c
        lax.fori_loop(0, DISPATCH_CHUNK, issue, 0)

        @pl.when(ci > 0)
        def _():
            wait_chunk()
        return carry

    lax.fori_loop(0, n_chunks, chunk, 0)
    wait_chunk()


def dispatch(h, pos, p_rows):
    n, d = h.shape
    assert n % DISPATCH_CHUNK == 0
    return pl.pallas_call(
        functools.partial(_dispatch_kernel, n=n),
        in_specs=[pl.BlockSpec(memory_space=pltpu.SMEM),
                  pl.BlockSpec(memory_space=pl.ANY),
                  pl.BlockSpec(memory_space=pl.ANY)],
        out_specs=pl.BlockSpec(memory_space=pl.ANY),
        out_shape=jax.ShapeDtypeStruct((p_rows, d), F32),
        scratch_shapes=[pltpu.SemaphoreType.DMA(())],
        input_output_aliases={2: 0},
        name="moe_dispatch",
    )(pos, h, jnp.zeros((p_rows, d), F32))


def _combine_kernel(pos_ref, x_ref, rt_ref, y_ref, o_ref, buf_ref, sem, *, n):
    base = pl.program_id(0) * COMBINE_ROWS

    def issue(t, c):
        for k in range(TOP_K):
            _row_copy(y_ref, pos_ref[k * n + base + t], buf_ref.at[k], t, sem).start()
        return c

    lax.fori_loop(0, COMBINE_ROWS, issue, 0)

    def w(t, c):
        _row_copy(y_ref, 0, buf_ref.at[0], 0, sem).wait()
        return c

    lax.fori_loop(0, TOP_K * COMBINE_ROWS, w, 0)
    rt = rt_ref[...]
    o_ref[...] = x_ref[...] + rt[:, 2:3] * buf_ref[0] + rt[:, 3:4] * buf_ref[1]


def combine(x, rt, y, pos):
    n, d = x.shape
    assert n % COMBINE_ROWS == 0
    grid_spec = pltpu.PrefetchScalarGridSpec(
        num_scalar_prefetch=1,
        grid=(n // COMBINE_ROWS,),
        in_specs=[pl.BlockSpec((COMBINE_ROWS, d), lambda i, pos: (i, 0)),
                  pl.BlockSpec((COMBINE_ROWS, LANE), lambda i, pos: (i, 0)),
                  pl.BlockSpec(memory_space=pl.ANY)],
        out_specs=pl.BlockSpec((COMBINE_ROWS, d), lambda i, pos: (i, 0)),
        scratch_shapes=[pltpu.VMEM((TOP_K, COMBINE_ROWS, d), F32), pltpu.SemaphoreType.DMA(())])
    return pl.pallas_call(
        functools.partial(_combine_kernel, n=n),
        grid_spec=grid_spec,
        out_shape=jax.ShapeDtypeStruct((n, d), F32),
        compiler_params=_cparams(("arbitrary",)),
        name="moe_combine",
    )(pos, x, rt, y)


def _route_plan(rt, tm, n_tiles):
    n = rt.shape[0]
    e_flat = jnp.concatenate([rt[:, 0], rt[:, 1]]).astype(jnp.int32)
    onehot = (e_flat[:, None] == jnp.arange(N_EXPERTS)[None, :]).astype(jnp.int32)
    csum = jnp.cumsum(onehot, axis=0)
    rank = jnp.sum(csum * onehot, axis=1) - 1
    counts = csum[-1]
    padded = ((counts + tm - 1) // tm) * tm
    gend = jnp.cumsum(padded)
    gstart = gend - padded
    pos = jnp.sum(onehot * gstart[None, :], axis=1) + rank
    nu = (gend[-1] // tm).astype(jnp.int32).reshape(1)
    tile_start = jnp.arange(n_tiles, dtype=jnp.int32) * tm
    te = jnp.sum((tile_start[:, None] >= gend[None, :]).astype(jnp.int32), axis=1)
    last = jnp.max(jnp.where(counts > 0, jnp.arange(N_EXPERTS), 0))
    te = jnp.minimum(te, last).astype(jnp.int32)
    return pos.astype(jnp.int32), te, nu


def ffn_moe(x, g, wr, wg, wu, wd):
    n, d = x.shape
    tm = _pick(n, (512, 256, 128))
    n_tiles = (TOP_K * n) // tm + N_EXPERTS
    h, rt = route_tokens(x, g, wr)
    pos, te, nu = _route_plan(rt, tm, n_tiles)
    xs = dispatch(h, pos, n_tiles * tm)
    act = ffn_up(xs, g, wg, wu, te, nu, tm, do_norm=False)
    y = ffn_down(act, wd, te, nu, tm, None)
    return combine(x, rt, y, pos)


def ffn_dense(x, g, wg, wu, wd):
    n, d = x.shape
    tm = _pick(n, (1024, 512, 256, 128))
    te = jnp.zeros((n // tm,), jnp.int32)
    nu = jnp.full((1,), n // tm, jnp.int32)
    act = ffn_up(x, g, wg[None], wu[None], te, nu, tm, do_norm=True)
    return ffn_down(act, wd[None], te, nu, tm, x)


def kernel(x_prompt, x_sample, cache_k, cache_v, page_table, state_hgrn, state_rglru, state_conv,
           w_in, w_out, g_mix, g_ffn, q_norm, k_norm, lam_q1, lam_k1, lam_q2, lam_k2, subln,
           rel_table, hgrn_lb, hgrn_onorm, conv_w, conv_b, rg_wa, rg_ba, rg_wx, rg_bx, rg_lambda,
           ff_gate, ff_up, ff_down, router, ex_gate, ex_up, ex_down):
    B, S, D = x_prompt.shape
    DB, Tq, _ = x_sample.shape
    depth = w_in.shape[0]
    NP, NS = B * S, DB * Tq
    AW = A_HEADS * LANE
    CW = C_BLOCKS * LANE
    xc0 = 3 * AW + 4 * B_HEADS * LANE
    x = jnp.concatenate([x_prompt.reshape(NP, D), x_sample.reshape(NS, D)], axis=0)

    sm = jax.nn.softmax(hgrn_lb.astype(F32), axis=0)
    lb_all = jnp.cumsum(sm, axis=0) - sm[0:1]

    outs = {k: [] for k in ("kp", "vp", "sp", "hp", "bp", "ks", "vs", "ss", "hs", "bs")}
    for l in range(depth):
        lam_init = 0.8 - 0.6 * math.exp(-0.3 * l)
        lamp = jnp.stack([lam_q1[l], lam_k1[l], lam_q2[l], lam_k2[l]]).astype(F32)
        proj = norm_matmul(x, g_mix[l], w_in[l].astype(BF16))
        qf, kf, qb, kb, vb = qk_prep(proj, q_norm[l], k_norm[l])
        vf = proj[:, 2 * AW:3 * AW]
        oa_p = attn_prompt(qb, kb, vb, rel_table, lamp, subln[l], B, S, lam_init)
        kn2 = kf[NP:].reshape(DB, Tq * A_HEADS, LANE)
        vn2 = vf[NP:].reshape(DB, Tq * A_HEADS, LANE)
        oa_s = attn_sample(qf, kn2, vn2, cache_k, cache_v, l, page_table, rel_table, lamp, subln[l],
                           NP, DB, Tq, lam_init)
        oa = jnp.concatenate([oa_p, oa_s.astype(BF16)], axis=0)
        ob_p, s_p = hgrn(proj, lb_all[l], hgrn_onorm[l], 0, B, S, None, l)
        ob_s, s_s = hgrn(proj, lb_all[l], hgrn_onorm[l], NP, DB, Tq, state_hgrn, l)
        ob = jnp.concatenate([ob_p, ob_s], axis=0)
        wts = (conv_w[l], conv_b[l].reshape(1, CW), rg_wa[l], rg_wx[l], rg_ba[l].reshape(1, CW),
               rg_bx[l].reshape(1, CW), rg_lambda[l].reshape(1, CW))
        oc_p, h_p = rglru_prompt(proj, wts, B, S)
        pbuf = jnp.concatenate([jnp.zeros((DB, Tq - (CONV_W - 1), CW), F32), state_conv[l]], axis=1).reshape(NS, CW)
        h0rep = jnp.repeat(state_rglru[l], Tq, axis=0)
        oc_s, h_s = rglru_sample(proj, wts, pbuf, h0rep, NP, NS, Tq)
        oc = jnp.concatenate([oc_p, oc_s], axis=0)
        x = out_proj(x, oa, ob, oc, w_out[l].astype(BF16))
        m = l // 2
        if l % 2 == 0:
            x = ffn_dense(x, g_ffn[l], ff_gate[m].astype(BF16), ff_up[m].astype(BF16), ff_down[m].astype(BF16))
        else:
            x = ffn_moe(x, g_ffn[l], router[m], ex_gate[m].astype(BF16), ex_up[m].astype(BF16),
                        ex_down[m].astype(BF16))
        xcs = proj[:, xc0:xc0 + CW]
        outs["kp"].append(kf[:NP].reshape(B, S, A_HEADS, LANE))
        outs["vp"].append(vf[:NP].reshape(B, S, A_HEADS, LANE))
        outs["sp"].append(s_p)
        outs["hp"].append(h_p.reshape(B, CW))
        outs["bp"].append(xcs[:NP].reshape(B, S, CW)[:, S - (CONV_W - 1):])
        outs["ks"].append(kf[NP:].reshape(DB, Tq, A_HEADS, LANE))
        outs["vs"].append(vf[NP:].reshape(DB, Tq, A_HEADS, LANE))
        outs["ss"].append(s_s)
        outs["hs"].append(h_s.reshape(DB, Tq, CW)[:, Tq - 1])
        outs["bs"].append(xcs[NP:].reshape(DB, Tq, CW)[:, Tq - (CONV_W - 1):])
    st = {k: jnp.stack(v) for k, v in outs.items()}
    return (x[:NP].reshape(B, S, D), x[NP:].reshape(DB, Tq, D),
            st["kp"], st["vp"], st["sp"], st["hp"], st["bp"],
            st["ks"], st["vs"], st["ss"], st["hs"], st["bs"])
```

```python
import functools
import math

import jax
import jax.numpy as jnp
from jax import lax
from jax.experimental import pallas as pl
from jax.experimental.pallas import tpu as pltpu

F32 = jnp.float32
BF16 = jnp.bfloat16
HI = lax.Precision.HIGHEST
EPS = 1e-6
NEG = -1e30
LANE = 128
SUBLANE = 8
NT = (((1,), (1,)), ((), ()))
TN = (((0,), (0,)), ((), ()))

A_HEADS = 8
A_SUB = 64
A_VDIM = 128
A_SCALE = A_SUB ** -0.5
LOG2E = math.log2(math.e)
Q_SCALE = A_SCALE * LOG2E
B_HEADS = 4
C_BLOCKS = 4
CONV_W = 4
RG_C = 8.0
PAGE = 128
N_BUCKETS = 32
MAX_EXACT = 16
MAX_DISTANCE = 128
N_EXPERTS = 8
TOP_K = 2


def _cparams(sem, vmem_mb=48):
    return pltpu.CompilerParams(dimension_semantics=sem, vmem_limit_bytes=vmem_mb << 20)


def _pick(n, prefs):
    for p in prefs:
        if n % p == 0:
            return p
    raise ValueError(f"no tile in {prefs} divides {n}")


def _rms(x, g):
    return x * lax.rsqrt(jnp.mean(x * x, axis=-1, keepdims=True) + EPS) * g


def _norm_matmul_kernel(x_ref, g_ref, w_ref, o_ref, h_ref):
    @pl.when(pl.program_id(1) == 0)
    def _():
        h_ref[...] = _rms(x_ref[...], g_ref[...]).astype(BF16)

    o_ref[...] = jnp.dot(h_ref[...], w_ref[...], preferred_element_type=F32)


def norm_matmul(x, g, w):
    n, d = x.shape
    c = w.shape[1]
    tm = _pick(n, (1024, 512, 256, 128))
    tn = _pick(c, (512, 256, 128))
    return pl.pallas_call(
        _norm_matmul_kernel,
        grid=(n // tm, c // tn),
        in_specs=[pl.BlockSpec((tm, d), lambda i, j: (i, 0)),
                  pl.BlockSpec((1, d), lambda i, j: (0, 0)),
                  pl.BlockSpec((d, tn), lambda i, j: (0, j))],
        out_specs=pl.BlockSpec((tm, tn), lambda i, j: (i, j)),
        out_shape=jax.ShapeDtypeStruct((n, c), F32),
        scratch_shapes=[pltpu.VMEM((tm, d), BF16)],
        compiler_params=_cparams(("parallel", "arbitrary")),
        name="norm_matmul",
    )(x, g.reshape(1, d), w)


def _qk_prep_kernel(q_ref, k_ref, v_ref, qg_ref, kg_ref, qf_ref, kf_ref, kb_ref, qt_ref, vt_ref):
    r = lax.broadcasted_iota(jnp.int32, (LANE, LANE), 0)
    c = lax.broadcasted_iota(jnp.int32, (LANE, LANE), 1)
    seg = jnp.where((r // A_SUB) == (c // A_SUB), 1.0 / A_SUB, 0.0).astype(F32)

    def normed(src, g_ref, sl):
        x = src[:, sl]
        ms = jnp.dot(x * x, seg, precision=HI, preferred_element_type=F32)
        return x * lax.rsqrt(ms + EPS) * g_ref[...]

    for h in range(A_HEADS):
        sl = slice(h * LANE, (h + 1) * LANE)
        q = normed(q_ref, qg_ref, sl) * Q_SCALE
        qf_ref[:, sl] = q
        qt_ref[h] = q.T.astype(BF16)
        k = normed(k_ref, kg_ref, sl)
        kf_ref[:, sl] = k
        kb_ref[:, sl] = k.astype(BF16)
        vt_ref[h] = v_ref[:, sl].T.astype(BF16)


def qk_prep(proj, q_norm, k_norm, tm):
    n = proj.shape[0]
    w = A_HEADS * LANE
    qg = jnp.concatenate([q_norm, q_norm]).reshape(1, LANE)
    kg = jnp.concatenate([k_norm, k_norm]).reshape(1, LANE)
    col = lambda cb: pl.BlockSpec((tm, w), lambda i, cb=cb: (i, cb))
    vec = pl.BlockSpec((1, LANE), lambda i: (0, 0))
    out = pl.BlockSpec((tm, w), lambda i: (i, 0))
    out_t = pl.BlockSpec((A_HEADS, None, LANE, tm), lambda i: (0, i, 0, 0))
    t_shape = jax.ShapeDtypeStruct((A_HEADS, n // tm, LANE, tm), BF16)
    return pl.pallas_call(
        _qk_prep_kernel,
        grid=(n // tm,),
        in_specs=[col(0), col(1), col(2), vec, vec],
        out_specs=[out, out, out, out_t, out_t],
        out_shape=[jax.ShapeDtypeStruct((n, w), F32)] * 2 + [jax.ShapeDtypeStruct((n, w), BF16), t_shape, t_shape],
        compiler_params=_cparams(("parallel",)),
        name="qk_prep",
    )(proj, proj, proj, qg, kg)


def _rel_bias_table(table, nmax):
    n = jnp.arange(nmax)
    nf = jnp.maximum(n, 1).astype(F32)
    far = MAX_EXACT + (jnp.log(nf / MAX_EXACT) / math.log(MAX_DISTANCE / MAX_EXACT)
                       * (N_BUCKETS - MAX_EXACT)).astype(jnp.int32)
    bucket = jnp.where(n < MAX_EXACT, n, jnp.minimum(far, N_BUCKETS - 1))
    onehot = (bucket[None, :] == jnp.arange(N_BUCKETS)[:, None]).astype(F32)
    t = table.astype(F32)
    vals = jnp.dot(t.T, onehot, precision=HI)
    return (vals - t[N_BUCKETS - 1][:, None]) * LOG2E


def _toeplitz(f, T):
    a = f[:, ::-1]
    flat = jnp.tile(a, (1, T))[:, :2 * T * T]
    return flat.reshape(f.shape[0], T, 2 * T)[:, :, T:]


def _lam_from(lamp, lam_init):
    s1 = jnp.sum(lamp[0:1] * lamp[1:2], axis=-1, keepdims=True)
    s2 = jnp.sum(lamp[2:3] * lamp[3:4], axis=-1, keepdims=True)
    return jnp.exp(s1) - jnp.exp(s2) + lam_init


def _attn_prompt_kernel(q_ref, k_ref, v_ref, bias_ref, lamp_ref, subln_ref, obuf_ref, o_ref,
                        m_ref, l_ref, acc_ref, *, T, lam_init):
    del obuf_ref
    qi = pl.program_id(2)
    qt = q_ref[...]
    sub = lax.broadcasted_iota(jnp.int32, qt.shape, 0)
    zero = jnp.zeros_like(qt)
    qs = (jnp.where(sub < A_SUB, qt, zero), jnp.where(sub >= A_SUB, qt, zero))
    m_ref[...] = jnp.full(m_ref.shape, NEG, F32)
    l_ref[...] = jnp.zeros(l_ref.shape, F32)
    acc_ref[...] = jnp.zeros(acc_ref.shape, F32)

    def tile(kt, bias):
        k = k_ref[pl.ds(pl.multiple_of(kt * T, T), T), :]
        vt = v_ref[kt]
        for c in range(2):
            s = jnp.dot(k, qs[c], preferred_element_type=F32)
            if bias is not None:
                s = s + bias
            m_old = m_ref[c]
            m_new = jnp.maximum(m_old, jnp.max(s, axis=0, keepdims=True))
            alpha = jnp.exp2(m_old - m_new)
            p = jnp.exp2(s - m_new)
            l_ref[c] = alpha * l_ref[c] + jnp.sum(p, axis=0, keepdims=True)
            acc_ref[c] = alpha * acc_ref[c] + jnp.dot(vt, p.astype(BF16), preferred_element_type=F32)
            m_ref[c] = m_new

    def far_body(kt, carry):
        tile(kt, None)
        return carry

    lax.fori_loop(0, jnp.maximum(qi - 1, 0), far_body, 0)

    @pl.when(qi >= 1)
    def _():
        tile(qi - 1, bias_ref[1])

    tile(qi, bias_ref[0])

    lam = _lam_from(lamp_ref[...], lam_init)
    ot = acc_ref[0] / l_ref[0] - lam * (acc_ref[1] / l_ref[1])
    ms = jnp.mean(ot * ot, axis=0, keepdims=True)
    y = ot * lax.rsqrt(ms + EPS) * subln_ref[...] * (1.0 - lam_init)
    o_ref[...] = y.T.astype(BF16)


def attn_prompt(qt, kb, vt, rel_table, lamp, subln, o_buf, B, S, T, lam_init):
    assert T >= MAX_DISTANCE and S % T == 0
    nq = S // T
    bt = _rel_bias_table(rel_table, 2 * T + 1)
    f0 = jnp.concatenate([jnp.full((A_HEADS, T), NEG, F32), bt[:, :T + 1]], axis=1)
    bias = jnp.stack([_toeplitz(f0, T), _toeplitz(bt, T)], axis=1).swapaxes(2, 3)
    return pl.pallas_call(
        functools.partial(_attn_prompt_kernel, T=T, lam_init=lam_init),
        grid=(B, A_HEADS, nq),
        in_specs=[pl.BlockSpec((None, None, LANE, T), lambda b, h, i: (h, b * nq + i, 0, 0)),
                  pl.BlockSpec((S, LANE), lambda b, h, i: (b, h)),
                  pl.BlockSpec((None, nq, LANE, T), lambda b, h, i: (h, b, 0, 0)),
                  pl.BlockSpec((None, 2, T, T), lambda b, h, i: (h, 0, 0, 0)),
                  pl.BlockSpec((4, A_SUB), lambda b, h, i: (0, 0)),
                  pl.BlockSpec((LANE, 1), lambda b, h, i: (0, 0)),
                  pl.BlockSpec(memory_space=pl.ANY)],
        out_specs=pl.BlockSpec((T, LANE), lambda b, h, i: (b * nq + i, h)),
        out_shape=jax.ShapeDtypeStruct(o_buf.shape, BF16),
        input_output_aliases={6: 0},
        scratch_shapes=[pltpu.VMEM((2, 1, T), F32), pltpu.VMEM((2, 1, T), F32),
                        pltpu.VMEM((2, LANE, T), F32)],
        compiler_params=_cparams(("parallel", "parallel", "arbitrary")),
        name="attn_prompt",
    )(qt, kb, vt, bias, lamp, subln.reshape(LANE, 1), o_buf)


def _attn_sample_kernel(pt_ref, q_ref, kn_ref, vn_ref, mask_ref, bnew_ref, lamp_ref, subln_ref, *rest,
                        PPS, n_steps, lam_init):
    k_refs = rest[:PPS]
    v_refs = rest[PPS:2 * PPS]
    o_ref = rest[2 * PPS + 1]
    qm_ref, m_ref, l_ref, acc_ref, s_ref = rest[2 * PPS + 2:]
    L = PAGE * A_HEADS
    step = pl.program_id(1)
    Tq = q_ref.shape[0]

    @pl.when(step == 0)
    def _():
        q = q_ref[...]
        lane = lax.broadcasted_iota(jnp.int32, (Tq, LANE), 1)
        rows = []
        for h in range(A_HEADS):
            qh = q[:, h * LANE:(h + 1) * LANE]
            rows.append(jnp.where(lane < A_SUB, qh, 0.0))
            rows.append(jnp.where(lane >= A_SUB, qh, 0.0))
        qm_ref[...] = jnp.concatenate(rows, axis=0).astype(BF16)
        m_ref[...] = jnp.full(m_ref.shape, NEG, F32)
        l_ref[...] = jnp.zeros(l_ref.shape, F32)
        acc_ref[...] = jnp.zeros(acc_ref.shape, F32)

    qm = qm_ref[...]
    mel = None
    for j in range(PPS):
        k2 = k_refs[j][...].reshape(L, LANE).astype(BF16)
        s = lax.dot_general(qm, k2, NT, preferred_element_type=F32)
        if j == PPS - 1:
            s = s + mask_ref[jnp.where(step == n_steps - 1, 1, 0)]
        else:
            s = s + mask_ref[0]
        s_ref[:, j * L:(j + 1) * L] = s
        mel = s if mel is None else jnp.maximum(mel, s)
    m_old = m_ref[...]
    m_new = jnp.maximum(m_old, jnp.max(mel, axis=-1, keepdims=True))
    alpha = jnp.exp2(m_old - m_new)
    lsum = None
    acc = alpha * acc_ref[...]
    for j in range(PPS):
        p = jnp.exp2(s_ref[:, j * L:(j + 1) * L] - m_new)
        lsum = p if lsum is None else lsum + p
        v2 = v_refs[j][...].reshape(L, LANE).astype(BF16)
        acc = acc + jnp.dot(p.astype(BF16), v2, preferred_element_type=F32)
    l_ref[...] = alpha * l_ref[...] + jnp.sum(lsum, axis=-1, keepdims=True)
    acc_ref[...] = acc
    m_ref[...] = m_new

    @pl.when(step == n_steps - 1)
    def _():
        kn2 = kn_ref[...].astype(BF16)
        vn2 = vn_ref[...].astype(BF16)
        s = lax.dot_general(qm, kn2, NT, preferred_element_type=F32) + bnew_ref[...]
        m_fin = jnp.maximum(m_new, jnp.max(s, axis=-1, keepdims=True))
        a_fin = jnp.exp2(m_new - m_fin)
        p = jnp.exp2(s - m_fin)
        l_ref[...] = a_fin * l_ref[...] + jnp.sum(p, axis=-1, keepdims=True)
        acc_ref[...] = a_fin * acc_ref[...] + jnp.dot(p.astype(BF16), vn2, preferred_element_type=F32)
        lam = _lam_from(lamp_ref[...], lam_init)
        o = acc_ref[...] / l_ref[...]
        for h in range(A_HEADS):
            r0 = h * 2 * Tq
            oh = o[r0:r0 + Tq] - lam * o[r0 + Tq:r0 + 2 * Tq]
            o_ref[:, h * LANE:(h + 1) * LANE] = (_rms(oh, subln_ref[...]) * (1.0 - lam_init)).astype(BF16)


def attn_sample(qf, kn2, vn2, cache_k, cache_v, layer, page_table, rel_table, lamp, subln, o_buf,
                NP, DB, Tq, lam_init):
    n_pages = page_table.shape[1]
    PPS = _pick(n_pages, (8, 4, 2, 1))
    n_steps = n_pages // PPS
    assert Tq <= PAGE and PAGE >= MAX_DISTANCE
    R = A_HEADS * 2 * Tq
    L = PAGE * A_HEADS
    bt = _rel_bias_table(rel_table, 2 * PAGE + 1)
    b_last = _toeplitz(bt, PAGE)[:, :Tq, :]
    f_new = jnp.concatenate([jnp.full((A_HEADS, PAGE), NEG, F32), bt[:, :PAGE + 1]], axis=1)
    b_new = _toeplitz(f_new, PAGE)[:, :Tq, :Tq]
    same = (jnp.arange(A_HEADS)[:, None] == jnp.arange(A_HEADS)[None, :])[:, None, None, None, :]
    expand = lambda b, nk: jnp.where(same, jnp.broadcast_to(b[:, None, :, :, None], (A_HEADS, 2, Tq, nk, A_HEADS)),
                                     NEG).reshape(R, nk * A_HEADS)
    masks = jnp.stack([expand(jnp.zeros((A_HEADS, Tq, PAGE), F32), PAGE), expand(b_last, PAGE)])
    bnew = expand(b_new, Tq)

    pt_flat = page_table.reshape(-1).astype(jnp.int32)
    page_spec = lambda j: pl.BlockSpec(
        (None, None, PAGE, A_HEADS, LANE),
        lambda b, s, pt, j=j: (layer, pt[b * n_pages + s * PPS + j], 0, 0, 0))
    const2 = lambda shape: pl.BlockSpec(shape, lambda b, s, pt: (0,) * len(shape))
    in_specs = ([pl.BlockSpec((Tq, A_HEADS * LANE), lambda b, s, pt: (NP // Tq + b, 0)),
                 pl.BlockSpec((None, Tq * A_HEADS, LANE), lambda b, s, pt: (b, 0, 0)),
                 pl.BlockSpec((None, Tq * A_HEADS, LANE), lambda b, s, pt: (b, 0, 0)),
                 const2((2, R, L)), const2((R, Tq * A_HEADS)), const2((4, A_SUB)), const2((1, LANE))]
                + [page_spec(j) for j in range(PPS)] * 2 + [pl.BlockSpec(memory_space=pl.ANY)])
    grid_spec = pltpu.PrefetchScalarGridSpec(
        num_scalar_prefetch=1,
        grid=(DB, n_steps),
        in_specs=in_specs,
        out_specs=pl.BlockSpec((Tq, A_HEADS * LANE), lambda b, s, pt: (NP // Tq + b, 0)),
        scratch_shapes=[pltpu.VMEM((R, LANE), BF16), pltpu.VMEM((R, 1), F32), pltpu.VMEM((R, 1), F32),
                        pltpu.VMEM((R, LANE), F32), pltpu.VMEM((R, PPS * L), F32)])
    return pl.pallas_call(
        functools.partial(_attn_sample_kernel, PPS=PPS, n_steps=n_steps, lam_init=lam_init),
        grid_spec=grid_spec,
        out_shape=jax.ShapeDtypeStruct(o_buf.shape, BF16),
        input_output_aliases={8 + 2 * PPS: 0},
        compiler_params=_cparams(("parallel", "arbitrary")),
        name="attn_sample",
    )(pt_flat, qf, kn2, vn2, masks, bnew, lamp, subln.reshape(1, LANE),
      *([cache_k] * PPS), *([cache_v] * PPS), o_buf)


HG_C = 128


def _hgrn_masks(levels):
    row = jnp.arange(HG_C)[:, None]
    col = jnp.arange(HG_C)[None, :]
    blocks = [row >= col]
    for lv in range(levels):
        s = 1 << lv
        mid = (row // (2 * s)) * (2 * s) + (s - 1)
        second = (row % (2 * s)) >= s
        blocks.append(jnp.where(second, (col > mid) & (col <= row), (col > row) & (col <= mid)))
    return jnp.concatenate(blocks, axis=0).astype(BF16)


def _hgrn_kernel(*refs, R, levels, has_state):
    if has_state:
        q_ref, f_ref, i_ref, g_ref, lb_ref, on_ref, mk_ref, s0_ref, _, o_ref, sf_ref, S_ref = refs
    else:
        q_ref, f_ref, i_ref, g_ref, lb_ref, on_ref, mk_ref, _, o_ref, sf_ref, S_ref = refs
    c = pl.program_id(1)

    @pl.when(c == 0)
    def _():
        S_ref[...] = s0_ref[...] if has_state else jnp.zeros(S_ref.shape, F32)

    row = lax.broadcasted_iota(jnp.int32, (HG_C, HG_C), 0)
    col = lax.broadcasted_iota(jnp.int32, (HG_C, HG_C), 1)
    rvec = lax.broadcasted_iota(jnp.int32, (HG_C, LANE), 0)
    mk = mk_ref[...]
    dot = functools.partial(jnp.dot, preferred_element_type=F32)
    for h in range(B_HEADS):
        sl = slice(h * LANE, (h + 1) * LANE)

        def rows(ref):
            x = ref[:, sl]
            if R < HG_C:
                x = jnp.concatenate([x, jnp.zeros((HG_C - R, LANE), F32)], axis=0)
            return x

        lb = lb_ref[:, sl]
        f = lb + (1.0 - lb) * jax.nn.sigmoid(rows(f_ref))
        g = jnp.log(f)
        kk = 1.0 - f
        qq = jax.nn.silu(rows(q_ref))
        v = rows(i_ref).astype(BF16)
        if R < HG_C:
            live = rvec < R
            g = jnp.where(live, g, 0.0)
            kk = jnp.where(live, kk, 0.0)
            qq = jnp.where(live, qq, 0.0)
        g0 = g.astype(BF16)
        r1 = g - g0.astype(F32)
        g1 = r1.astype(BF16)
        g2 = (r1 - g1.astype(F32)).astype(BF16)
        e3 = dot(mk, jnp.concatenate([g0, g1, g2], axis=-1))
        e = e3[:, :LANE] + e3[:, LANE:2 * LANE] + e3[:, 2 * LANE:]
        b = e[:HG_C]
        A = jnp.where(row == col, jnp.sum(qq * kk, axis=-1, keepdims=True), 0.0)
        for lv in range(levels):
            s = 1 << lv
            x = jnp.exp(e[(lv + 1) * HG_C:(lv + 2) * HG_C])
            second = (rvec % (2 * s)) >= s
            ql = jnp.where(second, qq * x, 0.0).astype(BF16)
            kl = jnp.where(second, 0.0, kk * x).astype(BF16)
            al = lax.dot_general(ql, kl, NT, preferred_element_type=F32)
            A = A + jnp.where((row // (2 * s)) == (col // (2 * s)), al, 0.0)
        S = S_ref[h]
        o = dot((qq * jnp.exp(b)).astype(BF16), S.astype(BF16)) + dot(A.astype(BF16), v)
        b_last = b[HG_C - 1:HG_C]
        e_col = jnp.sum(jnp.where(row == col, jnp.exp(b_last), 0.0), axis=1, keepdims=True)
        kd = (kk * jnp.exp(b_last - b)).astype(BF16)
        S_new = e_col * S + lax.dot_general(kd, v, TN, preferred_element_type=F32)
        S_ref[h] = S_new
        on = _rms(o, on_ref[...]) * jax.nn.silu(rows(g_ref))
        o_ref[:, sl] = on[:R].astype(BF16)

        @pl.when(c == pl.num_programs(1) - 1)
        def _():
            sf_ref[h] = S_new


def hgrn(proj, lb, onorm, row0, nb, T, s0, layer, o_buf):
    R = min(T, HG_C)
    nc = T // R
    levels = int(math.log2(R))
    base = row0 // R
    BW = B_HEADS * LANE
    qcol = 3 * A_HEADS * LANE // BW
    mk = _hgrn_masks(levels)
    blk = lambda cb: pl.BlockSpec((R, BW), lambda b, c, cb=cb: (base + b * nc + c, cb))
    in_specs = [blk(qcol), blk(qcol + 1), blk(qcol + 2), blk(qcol + 3),
                pl.BlockSpec((1, BW), lambda b, c: (0, 0)),
                pl.BlockSpec((1, LANE), lambda b, c: (0, 0)),
                pl.BlockSpec(mk.shape, lambda b, c: (0, 0))]
    args = [proj, proj, proj, proj, lb.reshape(1, BW), onorm.reshape(1, LANE), mk]
    if s0 is not None:
        in_specs.append(pl.BlockSpec((None, None, B_HEADS, LANE, LANE), lambda b, c: (layer, b, 0, 0, 0)))
        args.append(s0)
    in_specs.append(pl.BlockSpec(memory_space=pl.ANY))
    args.append(o_buf)
    return pl.pallas_call(
        functools.partial(_hgrn_kernel, R=R, levels=levels, has_state=s0 is not None),
        grid=(nb, nc),
        in_specs=in_specs,
        out_specs=[pl.BlockSpec((R, BW), lambda b, c: (base + b * nc + c, 0)),
                   pl.BlockSpec((None, B_HEADS, LANE, LANE), lambda b, c: (b, 0, 0, 0))],
        out_shape=[jax.ShapeDtypeStruct(o_buf.shape, BF16),
                   jax.ShapeDtypeStruct((nb, B_HEADS, LANE, LANE), F32)],
        input_output_aliases={len(args) - 1: 0},
        scratch_shapes=[pltpu.VMEM((B_HEADS, LANE, LANE), F32)],
        compiler_params=_cparams(("parallel", "arbitrary")),
        name="hgrn",
    )(*args)


def _rg_gates(u, wa_ref, wx_ref, ba, bx, sp):
    ra, rx = [], []
    for n in range(C_BLOCKS):
        un = u[:, n * LANE:(n + 1) * LANE]
        ra.append(jnp.dot(un, wa_ref[n], precision=HI, preferred_element_type=F32))
        rx.append(jnp.dot(un, wx_ref[n], precision=HI, preferred_element_type=F32))
    r = jax.nn.sigmoid(jnp.concatenate(ra, axis=-1) + ba)
    ig = jax.nn.sigmoid(jnp.concatenate(rx, axis=-1) + bx)
    log_a = -RG_C * r * sp
    a = jnp.exp(log_a)
    z = 2.0 * log_a
    u2 = jnp.exp(z)
    em1 = jnp.where(u2 == 1.0, z, (u2 - 1.0) * z / jnp.log(u2))
    mult = jnp.sqrt(-em1)
    return a, mult, ig * u


def _scan_rows(a, b, t, n_steps):
    n = a.shape[0]
    for lv in range(n_steps):
        d = 1 << lv
        ok = t >= d
        a_s = jnp.where(ok, pltpu.roll(a, d, 0), 1.0)
        b_s = jnp.where(ok, pltpu.roll(b, d, 0), 0.0)
        b = a * b_s + b
        a = a * a_s
    return a, b


def _rglru_prompt_kernel(x_ref, gc_ref, cw_ref, cb_ref, wa_ref, wx_ref, ba_ref, bx_ref, lam_ref,
                         obuf_ref, o_ref, hf_ref, prev_ref, h_ref, *, Tc):
    del obuf_ref
    i = pl.program_id(1)

    @pl.when(i == 0)
    def _():
        prev_ref[...] = jnp.zeros(prev_ref.shape, F32)
        h_ref[...] = jnp.zeros(h_ref.shape, F32)

    x = x_ref[...]
    prev = prev_ref[...]
    t8 = lax.broadcasted_iota(jnp.int32, prev.shape, 0)
    cw = cw_ref[...]
    u = cb_ref[...] + x * cw[CONV_W - 1:CONV_W]
    for j in range(CONV_W - 1):
        d = CONV_W - 1 - j
        xs = pltpu.roll(x, d, 0)
        head = jnp.where(t8 >= d, xs[:SUBLANE], pltpu.roll(prev, d, 0))
        u = u + jnp.concatenate([head, xs[SUBLANE:]], axis=0) * cw[j:j + 1]
    prev_ref[...] = x[Tc - SUBLANE:]
    sp = jax.nn.softplus(-lam_ref[...])
    a, mult, iu = _rg_gates(u, wa_ref, wx_ref, ba_ref[...], bx_ref[...], sp)
    t = lax.broadcasted_iota(jnp.int32, a.shape, 0)
    mult = jnp.where((t == 0) & (i == 0), 1.0, mult)
    acc_a, acc_b = _scan_rows(a, mult * iu, t, int(math.log2(Tc)))
    h = acc_a * h_ref[0:1] + acc_b
    h_ref[...] = jnp.broadcast_to(h[Tc - 1:Tc], h_ref.shape)
    o_ref[...] = (h * jax.nn.gelu(gc_ref[...])).astype(BF16)

    @pl.when(i == pl.num_programs(1) - 1)
    def _():
        hf_ref[...] = h[Tc - 1:Tc]


def _rg_weight_specs(W, nidx):
    z = lambda shape: pl.BlockSpec(shape, lambda *_: (0,) * len(shape))
    return [z((CONV_W, W)), z((1, W)), z((C_BLOCKS, LANE, LANE)), z((C_BLOCKS, LANE, LANE)),
            z((1, W)), z((1, W)), z((1, W))]


def rglru_prompt(proj, wts, o_buf, B, S):
    W = C_BLOCKS * LANE
    Tc = _pick(S, (512, 256, 128))
    nt = S // Tc
    xcol = (3 * A_HEADS + 4 * B_HEADS) * LANE // W
    args = (proj, proj, *wts, o_buf)
    return pl.pallas_call(
        functools.partial(_rglru_prompt_kernel, Tc=Tc),
        grid=(B, nt),
        in_specs=[pl.BlockSpec((Tc, W), lambda b, i: (b * nt + i, xcol)),
                  pl.BlockSpec((Tc, W), lambda b, i: (b * nt + i, xcol + 1))] + _rg_weight_specs(W, 2)
                 + [pl.BlockSpec(memory_space=pl.ANY)],
        out_specs=[pl.BlockSpec((Tc, W), lambda b, i: (b * nt + i, 0)),
                   pl.BlockSpec((None, 1, W), lambda b, i: (b, 0, 0))],
        out_shape=[jax.ShapeDtypeStruct(o_buf.shape, BF16), jax.ShapeDtypeStruct((B, 1, W), F32)],
        input_output_aliases={len(args) - 1: 0},
        scratch_shapes=[pltpu.VMEM((SUBLANE, W), F32), pltpu.VMEM((SUBLANE, W), F32)],
        compiler_params=_cparams(("parallel", "arbitrary")),
        name="rglru_prompt",
    )(*args)


def _rglru_sample_kernel(x_ref, gc_ref, p_ref, h0_ref, cw_ref, cb_ref, wa_ref, wx_ref, ba_ref, bx_ref, lam_ref,
                         obuf_ref, o_ref, h_ref, *, Tq):
    del obuf_ref
    x = x_ref[...]
    n = x.shape[0]
    t = lax.broadcasted_iota(jnp.int32, x.shape, 0) % Tq
    p = p_ref[...]
    cw = cw_ref[...]
    u = cb_ref[...] + x * cw[CONV_W - 1:CONV_W]
    for j in range(CONV_W - 1):
        d = CONV_W - 1 - j
        shifted = jnp.where(t >= d, pltpu.roll(x, d, 0), pltpu.roll(p, n - (Tq - d), 0))
        u = u + shifted * cw[j:j + 1]
    sp = jax.nn.softplus(-lam_ref[...])
    a, mult, iu = _rg_gates(u, wa_ref, wx_ref, ba_ref[...], bx_ref[...], sp)
    acc_a, acc_b = _scan_rows(a, mult * iu, t, int(math.log2(Tq)))
    h = acc_a * h0_ref[...] + acc_b
    h_ref[...] = h
    o_ref[...] = (h * jax.nn.gelu(gc_ref[...])).astype(BF16)


def rglru_sample(proj, wts, pbuf, h0rep, o_buf, NP, NS, Tq):
    W = C_BLOCKS * LANE
    tr = _pick(NS, (256, 128, 64, 32, 16, 8))
    xcol = (3 * A_HEADS + 4 * B_HEADS) * LANE // W
    base = NP // tr
    loc = pl.BlockSpec((tr, W), lambda i: (i, 0))
    args = (proj, proj, pbuf, h0rep, *wts, o_buf)
    return pl.pallas_call(
        functools.partial(_rglru_sample_kernel, Tq=Tq),
        grid=(NS // tr,),
        in_specs=[pl.BlockSpec((tr, W), lambda i: (base + i, xcol)),
                  pl.BlockSpec((tr, W), lambda i: (base + i, xcol + 1)), loc, loc] + _rg_weight_specs(W, 1)
                 + [pl.BlockSpec(memory_space=pl.ANY)],
        out_specs=[pl.BlockSpec((tr, W), lambda i: (base + i, 0)), loc],
        out_shape=[jax.ShapeDtypeStruct(o_buf.shape, BF16), jax.ShapeDtypeStruct((NS, W), F32)],
        input_output_aliases={len(args) - 1: 0},
        compiler_params=_cparams(("parallel",)),
        name="rglru_sample",
    )(*args)


def _out_proj_kernel(x_ref, a_ref, b_ref, c_ref, wa_ref, wb_ref, wc_ref, o_ref):
    y = jnp.dot(a_ref[...], wa_ref[...], preferred_element_type=F32)
    y = y + jnp.dot(b_ref[...], wb_ref[...], preferred_element_type=F32)
    y = y + jnp.dot(c_ref[...], wc_ref[...], preferred_element_type=F32)
    o_ref[...] = x_ref[...] + y


def out_proj(x, oa, ob, oc, w):
    n, d = x.shape
    wa_, wb_, wc_ = oa.shape[1], ob.shape[1], oc.shape[1]
    assert wb_ == wc_ and wa_ % wb_ == 0
    tm = _pick(n, (1024, 512, 256, 128))
    tn = _pick(d, (512, 256, 128))
    return pl.pallas_call(
        _out_proj_kernel,
        grid=(n // tm, d // tn),
        in_specs=[pl.BlockSpec((tm, tn), lambda i, j: (i, j)),
                  pl.BlockSpec((tm, wa_), lambda i, j: (i, 0)),
                  pl.BlockSpec((tm, wb_), lambda i, j: (i, 0)),
                  pl.BlockSpec((tm, wc_), lambda i, j: (i, 0)),
                  pl.BlockSpec((wa_, tn), lambda i, j: (0, j)),
                  pl.BlockSpec((wb_, tn), lambda i, j: (wa_ // wb_, j)),
                  pl.BlockSpec((wc_, tn), lambda i, j: (wa_ // wb_ + 1, j))],
        out_specs=pl.BlockSpec((tm, tn), lambda i, j: (i, j)),
        out_shape=jax.ShapeDtypeStruct((n, d), F32),
        compiler_params=_cparams(("parallel", "arbitrary")),
        name="out_proj",
    )(x, oa, ob, oc, w, w, w)


def _ffn_up_kernel(te_ref, nu_ref, x_ref, g_ref, wg_ref, wu_ref, o_ref, h_ref, *, do_norm):
    i = pl.program_id(0)

    @pl.when(pl.program_id(1) == 0)
    def _():
        x = x_ref[...]
        if do_norm:
            x = _rms(x, g_ref[...])
        h_ref[...] = x.astype(BF16)

    @pl.when(i < nu_ref[0])
    def _():
        h = h_ref[...]
        a = jnp.dot(h, wg_ref[...], preferred_element_type=F32)
        b = jnp.dot(h, wu_ref[...], preferred_element_type=F32)
        o_ref[...] = (jax.nn.silu(a) * b).astype(BF16)

    @pl.when(i >= nu_ref[0])
    def _():
        o_ref[...] = jnp.zeros(o_ref.shape, BF16)


def ffn_up(x, g, wg, wu, te, nu, tm, do_norm):
    n, d = x.shape
    f = wg.shape[-1]
    tf = _pick(f, (512, 256, 128))
    grid_spec = pltpu.PrefetchScalarGridSpec(
        num_scalar_prefetch=2,
        grid=(n // tm, f // tf),
        in_specs=[pl.BlockSpec((tm, d), lambda i, j, te, nu: (i, 0)),
                  pl.BlockSpec((1, d), lambda i, j, te, nu: (0, 0)),
                  pl.BlockSpec((None, d, tf), lambda i, j, te, nu: (te[i], 0, j)),
                  pl.BlockSpec((None, d, tf), lambda i, j, te, nu: (te[i], 0, j))],
        out_specs=pl.BlockSpec((tm, tf), lambda i, j, te, nu: (i, j)),
        scratch_shapes=[pltpu.VMEM((tm, d), BF16)])
    return pl.pallas_call(
        functools.partial(_ffn_up_kernel, do_norm=do_norm),
        grid_spec=grid_spec,
        out_shape=jax.ShapeDtypeStruct((n, f), BF16),
        compiler_params=_cparams(("parallel", "arbitrary")),
        name="ffn_up",
    )(te, nu, x, g.reshape(1, d), wg, wu)


def _ffn_down_kernel(te_ref, nu_ref, a_ref, wd_ref, *rest, residual):
    i = pl.program_id(0)
    if residual:
        r_ref, o_ref = rest
    else:
        (o_ref,) = rest

    @pl.when(i < nu_ref[0])
    def _():
        y = jnp.dot(a_ref[...], wd_ref[...], preferred_element_type=F32)
        if residual:
            y = r_ref[...] + y
        o_ref[...] = y

    @pl.when(i >= nu_ref[0])
    def _():
        o_ref[...] = jnp.zeros(o_ref.shape, F32)


def ffn_down(act, wd, te, nu, tm, res):
    n, f = act.shape
    d = wd.shape[-1]
    tn = _pick(d, (512, 256, 128))
    in_specs = [pl.BlockSpec((tm, f), lambda i, j, te, nu: (i, 0)),
                pl.BlockSpec((None, f, tn), lambda i, j, te, nu: (te[i], 0, j))]
    args = [act, wd]
    if res is not None:
        in_specs.append(pl.BlockSpec((tm, tn), lambda i, j, te, nu: (i, j)))
        args.append(res)
    grid_spec = pltpu.PrefetchScalarGridSpec(
        num_scalar_prefetch=2,
        grid=(n // tm, d // tn),
        in_specs=in_specs,
        out_specs=pl.BlockSpec((tm, tn), lambda i, j, te, nu: (i, j)))
    return pl.pallas_call(
        functools.partial(_ffn_down_kernel, residual=res is not None),
        grid_spec=grid_spec,
        out_shape=jax.ShapeDtypeStruct((n, d), F32),
        compiler_params=_cparams(("parallel", "arbitrary")),
        name="ffn_down",
    )(te, nu, *args)


def _router_kernel(x_ref, g_ref, wr_ref, h_ref, rt_ref):
    h = _rms(x_ref[...], g_ref[...])
    h_ref[...] = h
    logits = jnp.dot(h, wr_ref[...], precision=HI, preferred_element_type=F32)
    lane = lax.broadcasted_iota(jnp.int32, logits.shape, 1)
    l1 = jnp.where(lane < N_EXPERTS, logits, -jnp.inf)
    m1 = jnp.max(l1, axis=-1, keepdims=True)
    i1 = jnp.min(jnp.where(l1 == m1, lane, LANE), axis=-1, keepdims=True)
    l2 = jnp.where(lane == i1, -jnp.inf, l1)
    m2 = jnp.max(l2, axis=-1, keepdims=True)
    i2 = jnp.min(jnp.where(l2 == m2, lane, LANE), axis=-1, keepdims=True)
    e = jnp.exp(m2 - m1)
    g1 = 1.0 / (1.0 + e)
    g2 = e / (1.0 + e)
    rt_ref[...] = jnp.where(lane == 0, i1.astype(F32),
                            jnp.where(lane == 1, i2.astype(F32),
                                      jnp.where(lane == 2, g1, jnp.where(lane == 3, g2, 0.0))))


def route_tokens(x, g, wr):
    n, d = x.shape
    tm = _pick(n, (512, 256, 128))
    wr_pad = jnp.zeros((d, LANE), F32).at[:, :N_EXPERTS].set(wr)
    return pl.pallas_call(
        _router_kernel,
        grid=(n // tm,),
        in_specs=[pl.BlockSpec((tm, d), lambda i: (i, 0)),
                  pl.BlockSpec((1, d), lambda i: (0, 0)),
                  pl.BlockSpec((d, LANE), lambda i: (0, 0))],
        out_specs=[pl.BlockSpec((tm, d), lambda i: (i, 0)), pl.BlockSpec((tm, LANE), lambda i: (i, 0))],
        out_shape=[jax.ShapeDtypeStruct((n, d), F32), jax.ShapeDtypeStruct((n, LANE), F32)],
        compiler_params=_cparams(("parallel",)),
        name="router",
    )(x, g.reshape(1, d), wr_pad)


DISPATCH_ROWS = 256
COMBINE_ROWS = 128


def _row_copy(src, i, dst, j, sem):
    return pltpu.make_async_copy(src.at[pl.ds(i, 1)], dst.at[pl.ds(j, 1)], sem)


def _gather_start(idx_ref, idx0, src_ref, dst_ref, sem, rows):
    def issue(t, c):
        _row_copy(src_ref, idx_ref[idx0 + t], dst_ref, t, sem).start()
        return c
    lax.fori_loop(0, rows, issue, 0)


def _gather_wait(src_ref, dst_ref, sem, rows):
    def w(t, c):
        _row_copy(src_ref, 0, dst_ref, 0, sem).wait()
        return c
    lax.fori_loop(0, rows, w, 0)


def _dispatch_kernel(src_ref, nrows_ref, h_ref, o_ref, buf_ref, sem):
    i = pl.program_id(0)
    R = DISPATCH_ROWS
    live = lambda s: s * R < nrows_ref[0]

    def start(s):
        _gather_start(src_ref, s * R, h_ref, buf_ref.at[s % 2], sem.at[s % 2], R)

    @pl.when(i == 0)
    def _():
        start(i)

    @pl.when((i + 1 < pl.num_programs(0)) & live(i + 1))
    def _():
        start(i + 1)

    @pl.when(live(i))
    def _():
        _gather_wait(h_ref, buf_ref.at[i % 2], sem.at[i % 2], R)
        o_ref[...] = buf_ref[i % 2].astype(BF16)

    @pl.when(jnp.logical_not(live(i)))
    def _():
        o_ref[...] = jnp.zeros(o_ref.shape, BF16)


def dispatch(h, src, nrows):
    n, d = h.shape
    p_rows = src.shape[0]
    assert p_rows % DISPATCH_ROWS == 0
    grid_spec = pltpu.PrefetchScalarGridSpec(
        num_scalar_prefetch=2,
        grid=(p_rows // DISPATCH_ROWS,),
        in_specs=[pl.BlockSpec(memory_space=pl.ANY)],
        out_specs=pl.BlockSpec((DISPATCH_ROWS, d), lambda i, src, nr: (i, 0)),
        scratch_shapes=[pltpu.VMEM((2, DISPATCH_ROWS, d), F32), pltpu.SemaphoreType.DMA((2,))])
    return pl.pallas_call(
        _dispatch_kernel,
        grid_spec=grid_spec,
        out_shape=jax.ShapeDtypeStruct((p_rows, d), BF16),
        compiler_params=_cparams(("arbitrary",)),
        name="moe_dispatch",
    )(src, nrows, h)


def _combine_kernel(pos_ref, x_ref, rt_ref, y_ref, o_ref, buf_ref, sem, *, n):
    i = pl.program_id(0)
    R = COMBINE_ROWS

    def start(s):
        for k in range(TOP_K):
            _gather_start(pos_ref, k * n + s * R, y_ref, buf_ref.at[s % 2, k], sem.at[s % 2], R)

    @pl.when(i == 0)
    def _():
        start(i)

    @pl.when(i + 1 < pl.num_programs(0))
    def _():
        start(i + 1)

    for k in range(TOP_K):
        _gather_wait(y_ref, buf_ref.at[i % 2, k], sem.at[i % 2], R)
    rt = rt_ref[...]
    o_ref[...] = x_ref[...] + rt[:, 2:3] * buf_ref[i % 2, 0] + rt[:, 3:4] * buf_ref[i % 2, 1]


def combine(x, rt, y, pos):
    n, d = x.shape
    assert n % COMBINE_ROWS == 0
    grid_spec = pltpu.PrefetchScalarGridSpec(
        num_scalar_prefetch=1,
        grid=(n // COMBINE_ROWS,),
        in_specs=[pl.BlockSpec((COMBINE_ROWS, d), lambda i, pos: (i, 0)),
                  pl.BlockSpec((COMBINE_ROWS, LANE), lambda i, pos: (i, 0)),
                  pl.BlockSpec(memory_space=pl.ANY)],
        out_specs=pl.BlockSpec((COMBINE_ROWS, d), lambda i, pos: (i, 0)),
        scratch_shapes=[pltpu.VMEM((2, TOP_K, COMBINE_ROWS, d), F32), pltpu.SemaphoreType.DMA((2,))])
    return pl.pallas_call(
        functools.partial(_combine_kernel, n=n),
        grid_spec=grid_spec,
        out_shape=jax.ShapeDtypeStruct((n, d), F32),
        compiler_params=_cparams(("arbitrary",)),
        name="moe_combine",
    )(pos, x, rt, y)


def _route_plan(rt, tm, n_tiles):
    n = rt.shape[0]
    e_flat = jnp.concatenate([rt[:, 0], rt[:, 1]]).astype(jnp.int32)
    onehot = (e_flat[:, None] == jnp.arange(N_EXPERTS)[None, :]).astype(jnp.int32)
    csum = jnp.cumsum(onehot, axis=0)
    rank = jnp.sum(csum * onehot, axis=1) - 1
    counts = csum[-1]
    padded = ((counts + tm - 1) // tm) * tm
    gend = jnp.cumsum(padded)
    gstart = gend - padded
    pos = jnp.sum(onehot * gstart[None, :], axis=1) + rank
    nu = (gend[-1] // tm).astype(jnp.int32).reshape(1)
    tile_start = jnp.arange(n_tiles, dtype=jnp.int32) * tm
    te = jnp.sum((tile_start[:, None] >= gend[None, :]).astype(jnp.int32), axis=1)
    last = jnp.max(jnp.where(counts > 0, jnp.arange(N_EXPERTS), 0))
    te = jnp.minimum(te, last).astype(jnp.int32)
    pos = pos.astype(jnp.int32)
    tok = jnp.arange(TOP_K * n, dtype=jnp.int32) % n
    src = jnp.zeros((n_tiles * tm,), jnp.int32).at[pos].set(tok)
    return pos, src, te, nu


def ffn_moe(x, g, wr, wg, wu, wd):
    n, d = x.shape
    tm = _pick(n, (512, 256, 128))
    n_tiles = (TOP_K * n) // tm + N_EXPERTS
    h, rt = route_tokens(x, g, wr)
    pos, src, te, nu = _route_plan(rt, tm, n_tiles)
    xs = dispatch(h, src, nu * tm)
    act = ffn_up(xs, g, wg, wu, te, nu, tm, do_norm=False)
    y = ffn_down(act, wd, te, nu, tm, None)
    return combine(x, rt, y, pos)


def ffn_dense(x, g, wg, wu, wd):
    n, d = x.shape
    tm = _pick(n, (1024, 512, 256, 128))
    te = jnp.zeros((n // tm,), jnp.int32)
    nu = jnp.full((1,), n // tm, jnp.int32)
    act = ffn_up(x, g, wg[None], wu[None], te, nu, tm, do_norm=True)
    return ffn_down(act, wd[None], te, nu, tm, x)


def kernel(x_prompt, x_sample, cache_k, cache_v, page_table, state_hgrn, state_rglru, state_conv,
           w_in, w_out, g_mix, g_ffn, q_norm, k_norm, lam_q1, lam_k1, lam_q2, lam_k2, subln,
           rel_table, hgrn_lb, hgrn_onorm, conv_w, conv_b, rg_wa, rg_ba, rg_wx, rg_bx, rg_lambda,
           ff_gate, ff_up, ff_down, router, ex_gate, ex_up, ex_down):
    B, S, D = x_prompt.shape
    DB, Tq, _ = x_sample.shape
    depth = w_in.shape[0]
    NP, NS = B * S, DB * Tq
    AW = A_HEADS * LANE
    CW = C_BLOCKS * LANE
    xc0 = 3 * AW + 4 * B_HEADS * LANE
    x = jnp.concatenate([x_prompt.reshape(NP, D), x_sample.reshape(NS, D)], axis=0)

    sm = jax.nn.softmax(hgrn_lb.astype(F32), axis=0)
    lb_all = jnp.cumsum(sm, axis=0) - sm[0:1]

    outs = {k: [] for k in ("kp", "vp", "sp", "hp", "bp", "ks", "vs", "ss", "hs", "bs")}
    for l in range(depth):
        lam_init = 0.8 - 0.6 * math.exp(-0.3 * l)
        lamp = jnp.stack([lam_q1[l], lam_k1[l], lam_q2[l], lam_k2[l]]).astype(F32)
        proj = norm_matmul(x, g_mix[l], w_in[l].astype(BF16))
        ta = _pick(math.gcd(S, NP + NS), (512, 256, 128))
        qf, kf, kb, qt, vt = qk_prep(proj, q_norm[l], k_norm[l], ta)
        vf = proj[:, 2 * AW:3 * AW]
        oa = attn_prompt(qt, kb, vt, rel_table, lamp, subln[l], jnp.zeros((NP + NS, AW), BF16), B, S, ta, lam_init)
        kn2 = kf[NP:].reshape(DB, Tq * A_HEADS, LANE)
        vn2 = vf[NP:].reshape(DB, Tq * A_HEADS, LANE)
        oa = attn_sample(qf, kn2, vn2, cache_k, cache_v, l, page_table, rel_table, lamp, subln[l], oa,
                         NP, DB, Tq, lam_init)
        ob, s_p = hgrn(proj, lb_all[l], hgrn_onorm[l], 0, B, S, None, l, jnp.zeros((NP + NS, B_HEADS * LANE), BF16))
        ob, s_s = hgrn(proj, lb_all[l], hgrn_onorm[l], NP, DB, Tq, state_hgrn, l, ob)
        wts = (conv_w[l], conv_b[l].reshape(1, CW), rg_wa[l], rg_wx[l], rg_ba[l].reshape(1, CW),
               rg_bx[l].reshape(1, CW), rg_lambda[l].reshape(1, CW))
        oc, h_p = rglru_prompt(proj, wts, jnp.zeros((NP + NS, CW), BF16), B, S)
        pbuf = jnp.concatenate([jnp.zeros((DB, Tq - (CONV_W - 1), CW), F32), state_conv[l]], axis=1).reshape(NS, CW)
        h0rep = jnp.repeat(state_rglru[l], Tq, axis=0)
        oc, h_s = rglru_sample(proj, wts, pbuf, h0rep, oc, NP, NS, Tq)
        x = out_proj(x, oa, ob, oc, w_out[l].astype(BF16))
        m = l // 2
        if l % 2 == 0:
            x = ffn_dense(x, g_ffn[l], ff_gate[m].astype(BF16), ff_up[m].astype(BF16), ff_down[m].astype(BF16))
        else:
            x = ffn_moe(x, g_ffn[l], router[m], ex_gate[m].astype(BF16), ex_up[m].astype(BF16),
                        ex_down[m].astype(BF16))
        xcs = proj[:, xc0:xc0 + CW]
        outs["kp"].append(kf[:NP].reshape(B, S, A_HEADS, LANE))
        outs["vp"].append(vf[:NP].reshape(B, S, A_HEADS, LANE))
        outs["sp"].append(s_p)
        outs["hp"].append(h_p.reshape(B, CW))
        outs["bp"].append(xcs[:NP].reshape(B, S, CW)[:, S - (CONV_W - 1):])
        outs["ks"].append(kf[NP:].reshape(DB, Tq, A_HEADS, LANE))
        outs["vs"].append(vf[NP:].reshape(DB, Tq, A_HEADS, LANE))
        outs["ss"].append(s_s)
        outs["hs"].append(h_s.reshape(DB, Tq, CW)[:, Tq - 1])
        outs["bs"].append(xcs[NP:].reshape(DB, Tq, CW)[:, Tq - (CONV_W - 1):])
    st = {k: jnp.stack(v) for k, v in outs.items()}
    return (x[:NP].reshape(B, S, D), x[NP:].reshape(DB, Tq, D),
            st["kp"], st["vp"], st["sp"], st["hp"], st["bp"],
            st["ks"], st["vs"], st["ss"], st["hs"], st["bs"])
```

```python
import functools
import math

import jax
import jax.numpy as jnp
from jax import lax
from jax.experimental import pallas as pl
from jax.experimental.pallas import tpu as pltpu

F32 = jnp.float32
BF16 = jnp.bfloat16
HI = lax.Precision.HIGHEST
EPS = 1e-6
NEG = -1e30
LANE = 128
SUBLANE = 8
NT = (((1,), (1,)), ((), ()))
TN = (((0,), (0,)), ((), ()))

A_HEADS = 8
A_SUB = 64
A_VDIM = 128
A_SCALE = A_SUB ** -0.5
LOG2E = math.log2(math.e)
Q_SCALE = A_SCALE * LOG2E
B_HEADS = 4
C_BLOCKS = 4
CONV_W = 4
RG_C = 8.0
PAGE = 128
N_BUCKETS = 32
MAX_EXACT = 16
MAX_DISTANCE = 128
N_EXPERTS = 8
TOP_K = 2


def _cparams(sem, vmem_mb=48):
    return pltpu.CompilerParams(dimension_semantics=sem, vmem_limit_bytes=vmem_mb << 20)


def _pick(n, prefs):
    for p in prefs:
        if n % p == 0:
            return p
    raise ValueError(f"no tile in {prefs} divides {n}")


def _rms(x, g):
    return x * lax.rsqrt(jnp.mean(x * x, axis=-1, keepdims=True) + EPS) * g


def _norm_matmul_kernel(x_ref, g_ref, w_ref, o_ref, h_ref):
    @pl.when(pl.program_id(1) == 0)
    def _():
        h_ref[...] = _rms(x_ref[...], g_ref[...]).astype(BF16)

    o_ref[...] = jnp.dot(h_ref[...], w_ref[...], preferred_element_type=F32)


def norm_matmul(x, g, w):
    n, d = x.shape
    c = w.shape[1]
    tm = _pick(n, (1024, 512, 256, 128))
    tn = _pick(c, (512, 256, 128))
    return pl.pallas_call(
        _norm_matmul_kernel,
        grid=(n // tm, c // tn),
        in_specs=[pl.BlockSpec((tm, d), lambda i, j: (i, 0)),
                  pl.BlockSpec((1, d), lambda i, j: (0, 0)),
                  pl.BlockSpec((d, tn), lambda i, j: (0, j))],
        out_specs=pl.BlockSpec((tm, tn), lambda i, j: (i, j)),
        out_shape=jax.ShapeDtypeStruct((n, c), F32),
        scratch_shapes=[pltpu.VMEM((tm, d), BF16)],
        compiler_params=_cparams(("parallel", "arbitrary")),
        name="norm_matmul",
    )(x, g.reshape(1, d), w)


def _qk_prep_kernel(q_ref, k_ref, v_ref, qg_ref, kg_ref, qf_ref, kf_ref, kb_ref, qt_ref, vt_ref):
    r = lax.broadcasted_iota(jnp.int32, (LANE, LANE), 0)
    c = lax.broadcasted_iota(jnp.int32, (LANE, LANE), 1)
    seg = jnp.where((r // A_SUB) == (c // A_SUB), 1.0 / A_SUB, 0.0).astype(F32)

    def normed(src, g_ref, sl):
        x = src[:, sl]
        ms = jnp.dot(x * x, seg, precision=HI, preferred_element_type=F32)
        return x * lax.rsqrt(ms + EPS) * g_ref[...]

    for h in range(A_HEADS):
        sl = slice(h * LANE, (h + 1) * LANE)
        q = normed(q_ref, qg_ref, sl) * Q_SCALE
        qf_ref[:, sl] = q
        qt_ref[h] = q.T.astype(BF16)
        k = normed(k_ref, kg_ref, sl)
        kf_ref[:, sl] = k
        kb_ref[:, sl] = k.astype(BF16)
        vt_ref[h] = v_ref[:, sl].T.astype(BF16)


def qk_prep(proj, q_norm, k_norm, tm):
    n = proj.shape[0]
    w = A_HEADS * LANE
    qg = jnp.concatenate([q_norm, q_norm]).reshape(1, LANE)
    kg = jnp.concatenate([k_norm, k_norm]).reshape(1, LANE)
    col = lambda cb: pl.BlockSpec((tm, w), lambda i, cb=cb: (i, cb))
    vec = pl.BlockSpec((1, LANE), lambda i: (0, 0))
    out = pl.BlockSpec((tm, w), lambda i: (i, 0))
    out_t = pl.BlockSpec((A_HEADS, None, LANE, tm), lambda i: (0, i, 0, 0))
    t_shape = jax.ShapeDtypeStruct((A_HEADS, n // tm, LANE, tm), BF16)
    return pl.pallas_call(
        _qk_prep_kernel,
        grid=(n // tm,),
        in_specs=[col(0), col(1), col(2), vec, vec],
        out_specs=[out, out, out, out_t, out_t],
        out_shape=[jax.ShapeDtypeStruct((n, w), F32)] * 2 + [jax.ShapeDtypeStruct((n, w), BF16), t_shape, t_shape],
        compiler_params=_cparams(("parallel",)),
        name="qk_prep",
    )(proj, proj, proj, qg, kg)


def _rel_bias(table, n):
    nf = jnp.maximum(n, 1).astype(F32)
    far = MAX_EXACT + (jnp.log(nf / MAX_EXACT) / math.log(MAX_DISTANCE / MAX_EXACT)
                       * (N_BUCKETS - MAX_EXACT)).astype(jnp.int32)
    bucket = jnp.where(n < MAX_EXACT, n, jnp.minimum(far, N_BUCKETS - 1))
    onehot = (bucket.reshape(1, -1) == jnp.arange(N_BUCKETS)[:, None]).astype(F32)
    t = table.astype(F32)
    vals = jnp.dot(t.T, onehot, precision=HI)
    return ((vals - t[N_BUCKETS - 1][:, None]) * LOG2E).reshape((t.shape[1],) + n.shape)


def _lam_from(lamp, lam_init):
    s1 = jnp.sum(lamp[0:1] * lamp[1:2], axis=-1, keepdims=True)
    s2 = jnp.sum(lamp[2:3] * lamp[3:4], axis=-1, keepdims=True)
    return jnp.exp(s1) - jnp.exp(s2) + lam_init


def _attn_prompt_kernel(q_ref, k_ref, v_ref, bias_ref, lamp_ref, subln_ref, obuf_ref, o_ref,
                        m_ref, l_ref, acc_ref, *, T, lam_init):
    del obuf_ref
    qi = pl.program_id(2)
    qt = q_ref[...]
    sub = lax.broadcasted_iota(jnp.int32, qt.shape, 0)
    zero = jnp.zeros_like(qt)
    qs = (jnp.where(sub < A_SUB, qt, zero), jnp.where(sub >= A_SUB, qt, zero))
    m_ref[...] = jnp.full(m_ref.shape, NEG, F32)
    l_ref[...] = jnp.zeros(l_ref.shape, F32)
    acc_ref[...] = jnp.zeros(acc_ref.shape, F32)

    def tile(kt, bias):
        k = k_ref[pl.ds(pl.multiple_of(kt * T, T), T), :]
        vt = v_ref[kt]
        for c in range(2):
            s = jnp.dot(k, qs[c], preferred_element_type=F32)
            if bias is not None:
                s = s + bias
            m_old = m_ref[c]
            m_new = jnp.maximum(m_old, jnp.max(s, axis=0, keepdims=True))
            alpha = jnp.exp2(m_old - m_new)
            p = jnp.exp2(s - m_new)
            l_ref[c] = alpha * l_ref[c] + jnp.sum(p, axis=0, keepdims=True)
            acc_ref[c] = alpha * acc_ref[c] + jnp.dot(vt, p.astype(BF16), preferred_element_type=F32)
            m_ref[c] = m_new

    def far_body(kt, carry):
        tile(kt, None)
        return carry

    lax.fori_loop(0, jnp.maximum(qi - 1, 0), far_body, 0)

    nb = T // LANE
    same, after = bias_ref[0], bias_ref[1]
    zero_b = jnp.zeros((LANE, LANE), F32)
    neg_b = jnp.full((LANE, LANE), NEG, F32)
    grid_of = lambda pick: jnp.concatenate(
        [jnp.concatenate([pick(jk, iq) for iq in range(nb)], axis=1) for jk in range(nb)], axis=0)

    @pl.when(qi >= 1)
    def _():
        tile(qi - 1, grid_of(lambda jk, iq: after if (jk == nb - 1 and iq == 0) else zero_b))

    tile(qi, grid_of(lambda jk, iq: neg_b if iq < jk else same if iq == jk else after if iq == jk + 1 else zero_b))

    lam = _lam_from(lamp_ref[...], lam_init)
    ot = acc_ref[0] / l_ref[0] - lam * (acc_ref[1] / l_ref[1])
    ms = jnp.mean(ot * ot, axis=0, keepdims=True)
    y = ot * lax.rsqrt(ms + EPS) * subln_ref[...] * (1.0 - lam_init)
    o_ref[...] = y.T.astype(BF16)


def attn_prompt(qt, kb, vt, rel_table, lamp, subln, o_buf, B, S, T, lam_init):
    assert T >= MAX_DISTANCE and S % T == 0
    nq = S // T
    assert LANE == MAX_DISTANCE
    d = jnp.arange(LANE)[None, :] - jnp.arange(LANE)[:, None]
    bias = jnp.stack([jnp.where(d >= 0, _rel_bias(rel_table, jnp.maximum(d, 0)), NEG),
                      _rel_bias(rel_table, d + LANE)], axis=1)
    return pl.pallas_call(
        functools.partial(_attn_prompt_kernel, T=T, lam_init=lam_init),
        grid=(B, A_HEADS, nq),
        in_specs=[pl.BlockSpec((None, None, LANE, T), lambda b, h, i: (h, b * nq + i, 0, 0)),
                  pl.BlockSpec((S, LANE), lambda b, h, i: (b, h)),
                  pl.BlockSpec((None, nq, LANE, T), lambda b, h, i: (h, b, 0, 0)),
                  pl.BlockSpec((None, 2, LANE, LANE), lambda b, h, i: (h, 0, 0, 0)),
                  pl.BlockSpec((4, A_SUB), lambda b, h, i: (0, 0)),
                  pl.BlockSpec((LANE, 1), lambda b, h, i: (0, 0)),
                  pl.BlockSpec(memory_space=pl.ANY)],
        out_specs=pl.BlockSpec((T, LANE), lambda b, h, i: (b * nq + i, h)),
        out_shape=jax.ShapeDtypeStruct(o_buf.shape, BF16),
        input_output_aliases={6: 0},
        scratch_shapes=[pltpu.VMEM((2, 1, T), F32), pltpu.VMEM((2, 1, T), F32),
                        pltpu.VMEM((2, LANE, T), F32)],
        compiler_params=_cparams(("parallel", "parallel", "arbitrary")),
        name="attn_prompt",
    )(qt, kb, vt, bias, lamp, subln.reshape(LANE, 1), o_buf)


def _attn_sample_kernel(pt_ref, q_ref, kn_ref, vn_ref, mask_ref, bnew_ref, lamp_ref, subln_ref, *rest,
                        PPS, n_steps, lam_init):
    k_refs = rest[:PPS]
    v_refs = rest[PPS:2 * PPS]
    o_ref = rest[2 * PPS + 1]
    qm_ref, m_ref, l_ref, acc_ref, s_ref = rest[2 * PPS + 2:]
    L = PAGE * A_HEADS
    step = pl.program_id(1)
    Tq = q_ref.shape[0]

    @pl.when(step == 0)
    def _():
        q = q_ref[...]
        lane = lax.broadcasted_iota(jnp.int32, (Tq, LANE), 1)
        rows = []
        for h in range(A_HEADS):
            qh = q[:, h * LANE:(h + 1) * LANE]
            rows.append(jnp.where(lane < A_SUB, qh, 0.0))
            rows.append(jnp.where(lane >= A_SUB, qh, 0.0))
        qm_ref[...] = jnp.concatenate(rows, axis=0).astype(BF16)
        m_ref[...] = jnp.full(m_ref.shape, NEG, F32)
        l_ref[...] = jnp.zeros(l_ref.shape, F32)
        acc_ref[...] = jnp.zeros(acc_ref.shape, F32)

    qm = qm_ref[...]
    mel = None
    for j in range(PPS):
        k2 = k_refs[j][...].reshape(L, LANE).astype(BF16)
        s = lax.dot_general(qm, k2, NT, preferred_element_type=F32)
        if j == PPS - 1:
            s = s + mask_ref[jnp.where(step == n_steps - 1, 1, 0)]
        else:
            s = s + mask_ref[0]
        s_ref[:, j * L:(j + 1) * L] = s
        mel = s if mel is None else jnp.maximum(mel, s)
    m_old = m_ref[...]
    m_new = jnp.maximum(m_old, jnp.max(mel, axis=-1, keepdims=True))
    alpha = jnp.exp2(m_old - m_new)
    lsum = None
    acc = alpha * acc_ref[...]
    for j in range(PPS):
        p = jnp.exp2(s_ref[:, j * L:(j + 1) * L] - m_new)
        lsum = p if lsum is None else lsum + p
        v2 = v_refs[j][...].reshape(L, LANE).astype(BF16)
        acc = acc + jnp.dot(p.astype(BF16), v2, preferred_element_type=F32)
    l_ref[...] = alpha * l_ref[...] + jnp.sum(lsum, axis=-1, keepdims=True)
    acc_ref[...] = acc
    m_ref[...] = m_new

    @pl.when(step == n_steps - 1)
    def _():
        kn2 = kn_ref[...].astype(BF16)
        vn2 = vn_ref[...].astype(BF16)
        s = lax.dot_general(qm, kn2, NT, preferred_element_type=F32) + bnew_ref[...]
        m_fin = jnp.maximum(m_new, jnp.max(s, axis=-1, keepdims=True))
        a_fin = jnp.exp2(m_new - m_fin)
        p = jnp.exp2(s - m_fin)
        l_ref[...] = a_fin * l_ref[...] + jnp.sum(p, axis=-1, keepdims=True)
        acc_ref[...] = a_fin * acc_ref[...] + jnp.dot(p.astype(BF16), vn2, preferred_element_type=F32)
        lam = _lam_from(lamp_ref[...], lam_init)
        o = acc_ref[...] / l_ref[...]
        for h in range(A_HEADS):
            r0 = h * 2 * Tq
            oh = o[r0:r0 + Tq] - lam * o[r0 + Tq:r0 + 2 * Tq]
            o_ref[:, h * LANE:(h + 1) * LANE] = (_rms(oh, subln_ref[...]) * (1.0 - lam_init)).astype(BF16)


def attn_sample(qf, kn2, vn2, cache_k, cache_v, layer, page_table, rel_table, lamp, subln, o_buf,
                NP, DB, Tq, lam_init):
    n_pages = page_table.shape[1]
    PPS = _pick(n_pages, (16, 8, 4, 2, 1))
    n_steps = n_pages // PPS
    assert Tq <= PAGE and PAGE >= MAX_DISTANCE
    R = A_HEADS * 2 * Tq
    L = PAGE * A_HEADS
    t_ = jnp.arange(Tq)[:, None]
    b_last = _rel_bias(rel_table, PAGE + t_ - jnp.arange(PAGE)[None, :])
    d_new = t_ - jnp.arange(Tq)[None, :]
    b_new = jnp.where(d_new >= 0, _rel_bias(rel_table, jnp.maximum(d_new, 0)), NEG)
    same = (jnp.arange(A_HEADS)[:, None] == jnp.arange(A_HEADS)[None, :])[:, None, None, None, :]
    expand = lambda b, nk: jnp.where(same, jnp.broadcast_to(b[:, None, :, :, None], (A_HEADS, 2, Tq, nk, A_HEADS)),
                                     NEG).reshape(R, nk * A_HEADS)
    masks = jnp.stack([expand(jnp.zeros((A_HEADS, Tq, PAGE), F32), PAGE), expand(b_last, PAGE)])
    bnew = expand(b_new, Tq)

    pt_flat = page_table.reshape(-1).astype(jnp.int32)
    page_spec = lambda j: pl.BlockSpec(
        (None, None, PAGE, A_HEADS, LANE),
        lambda b, s, pt, j=j: (layer, pt[b * n_pages + s * PPS + j], 0, 0, 0))
    const2 = lambda shape: pl.BlockSpec(shape, lambda b, s, pt: (0,) * len(shape))
    in_specs = ([pl.BlockSpec((Tq, A_HEADS * LANE), lambda b, s, pt: (NP // Tq + b, 0)),
                 pl.BlockSpec((None, Tq * A_HEADS, LANE), lambda b, s, pt: (b, 0, 0)),
                 pl.BlockSpec((None, Tq * A_HEADS, LANE), lambda b, s, pt: (b, 0, 0)),
                 const2((2, R, L)), const2((R, Tq * A_HEADS)), const2((4, A_SUB)), const2((1, LANE))]
                + [page_spec(j) for j in range(PPS)] * 2 + [pl.BlockSpec(memory_space=pl.ANY)])
    grid_spec = pltpu.PrefetchScalarGridSpec(
        num_scalar_prefetch=1,
        grid=(DB, n_steps),
        in_specs=in_specs,
        out_specs=pl.BlockSpec((Tq, A_HEADS * LANE), lambda b, s, pt: (NP // Tq + b, 0)),
        scratch_shapes=[pltpu.VMEM((R, LANE), BF16), pltpu.VMEM((R, 1), F32), pltpu.VMEM((R, 1), F32),
                        pltpu.VMEM((R, LANE), F32), pltpu.VMEM((R, PPS * L), F32)])
    return pl.pallas_call(
        functools.partial(_attn_sample_kernel, PPS=PPS, n_steps=n_steps, lam_init=lam_init),
        grid_spec=grid_spec,
        out_shape=jax.ShapeDtypeStruct(o_buf.shape, BF16),
        input_output_aliases={8 + 2 * PPS: 0},
        compiler_params=_cparams(("parallel", "arbitrary"), vmem_mb=56),
        name="attn_sample",
    )(pt_flat, qf, kn2, vn2, masks, bnew, lamp, subln.reshape(1, LANE),
      *([cache_k] * PPS), *([cache_v] * PPS), o_buf)


HG_C = 128


def _hgrn_masks(levels):
    row = jnp.arange(HG_C)[:, None]
    col = jnp.arange(HG_C)[None, :]
    blocks = [row >= col]
    for lv in range(levels):
        s = 1 << lv
        mid = (row // (2 * s)) * (2 * s) + (s - 1)
        second = (row % (2 * s)) >= s
        blocks.append(jnp.where(second, (col > mid) & (col <= row), (col > row) & (col <= mid)))
    return jnp.concatenate(blocks, axis=0).astype(BF16)


def _hgrn_kernel(*refs, R, levels, has_state):
    if has_state:
        q_ref, f_ref, i_ref, g_ref, lb_ref, on_ref, mk_ref, s0_ref, _, o_ref, sf_ref, S_ref = refs
    else:
        q_ref, f_ref, i_ref, g_ref, lb_ref, on_ref, mk_ref, _, o_ref, sf_ref, S_ref = refs
    c = pl.program_id(1)

    @pl.when(c == 0)
    def _():
        S_ref[...] = s0_ref[...] if has_state else jnp.zeros(S_ref.shape, F32)

    row = lax.broadcasted_iota(jnp.int32, (HG_C, HG_C), 0)
    col = lax.broadcasted_iota(jnp.int32, (HG_C, HG_C), 1)
    rvec = lax.broadcasted_iota(jnp.int32, (HG_C, LANE), 0)
    mk = mk_ref[...]
    dot = functools.partial(jnp.dot, preferred_element_type=F32)
    for h in range(B_HEADS):
        sl = slice(h * LANE, (h + 1) * LANE)

        def rows(ref):
            x = ref[:, sl]
            if R < HG_C:
                x = jnp.concatenate([x, jnp.zeros((HG_C - R, LANE), F32)], axis=0)
            return x

        lb = lb_ref[:, sl]
        f = lb + (1.0 - lb) * jax.nn.sigmoid(rows(f_ref))
        g = jnp.log(f)
        kk = 1.0 - f
        qq = jax.nn.silu(rows(q_ref))
        v = rows(i_ref).astype(BF16)
        if R < HG_C:
            live = rvec < R
            g = jnp.where(live, g, 0.0)
            kk = jnp.where(live, kk, 0.0)
            qq = jnp.where(live, qq, 0.0)
        g0 = g.astype(BF16)
        r1 = g - g0.astype(F32)
        g1 = r1.astype(BF16)
        g2 = (r1 - g1.astype(F32)).astype(BF16)
        e3 = dot(mk, jnp.concatenate([g0, g1, g2], axis=-1))
        e = e3[:, :LANE] + e3[:, LANE:2 * LANE] + e3[:, 2 * LANE:]
        b = e[:HG_C]
        A = jnp.where(row == col, jnp.sum(qq * kk, axis=-1, keepdims=True), 0.0)
        for lv in range(levels):
            s = 1 << lv
            x = jnp.exp(e[(lv + 1) * HG_C:(lv + 2) * HG_C])
            second = (rvec % (2 * s)) >= s
            ql = jnp.where(second, qq * x, 0.0).astype(BF16)
            kl = jnp.where(second, 0.0, kk * x).astype(BF16)
            al = lax.dot_general(ql, kl, NT, preferred_element_type=F32)
            A = A + jnp.where((row // (2 * s)) == (col // (2 * s)), al, 0.0)
        S = S_ref[h]
        o = dot((qq * jnp.exp(b)).astype(BF16), S.astype(BF16)) + dot(A.astype(BF16), v)
        b_last = b[HG_C - 1:HG_C]
        e_col = jnp.sum(jnp.where(row == col, jnp.exp(b_last), 0.0), axis=1, keepdims=True)
        kd = (kk * jnp.exp(b_last - b)).astype(BF16)
        S_new = e_col * S + lax.dot_general(kd, v, TN, preferred_element_type=F32)
        S_ref[h] = S_new
        on = _rms(o, on_ref[...]) * jax.nn.silu(rows(g_ref))
        o_ref[:, sl] = on[:R].astype(BF16)

        @pl.when(c == pl.num_programs(1) - 1)
        def _():
            sf_ref[h] = S_new


def hgrn(proj, lb, onorm, row0, nb, T, s0, layer, o_buf):
    R = min(T, HG_C)
    nc = T // R
    levels = int(math.log2(R))
    base = row0 // R
    BW = B_HEADS * LANE
    qcol = 3 * A_HEADS * LANE // BW
    mk = _hgrn_masks(levels)
    blk = lambda cb: pl.BlockSpec((R, BW), lambda b, c, cb=cb: (base + b * nc + c, cb))
    in_specs = [blk(qcol), blk(qcol + 1), blk(qcol + 2), blk(qcol + 3),
                pl.BlockSpec((1, BW), lambda b, c: (0, 0)),
                pl.BlockSpec((1, LANE), lambda b, c: (0, 0)),
                pl.BlockSpec(mk.shape, lambda b, c: (0, 0))]
    args = [proj, proj, proj, proj, lb.reshape(1, BW), onorm.reshape(1, LANE), mk]
    if s0 is not None:
        in_specs.append(pl.BlockSpec((None, None, B_HEADS, LANE, LANE), lambda b, c: (layer, b, 0, 0, 0)))
        args.append(s0)
    in_specs.append(pl.BlockSpec(memory_space=pl.ANY))
    args.append(o_buf)
    return pl.pallas_call(
        functools.partial(_hgrn_kernel, R=R, levels=levels, has_state=s0 is not None),
        grid=(nb, nc),
        in_specs=in_specs,
        out_specs=[pl.BlockSpec((R, BW), lambda b, c: (base + b * nc + c, 0)),
                   pl.BlockSpec((None, B_HEADS, LANE, LANE), lambda b, c: (b, 0, 0, 0))],
        out_shape=[jax.ShapeDtypeStruct(o_buf.shape, BF16),
                   jax.ShapeDtypeStruct((nb, B_HEADS, LANE, LANE), F32)],
        input_output_aliases={len(args) - 1: 0},
        scratch_shapes=[pltpu.VMEM((B_HEADS, LANE, LANE), F32)],
        compiler_params=_cparams(("parallel", "arbitrary")),
        name="hgrn",
    )(*args)


def _rg_gates(u, wa_ref, wx_ref, ba, bx, sp):
    ra, rx = [], []
    for n in range(C_BLOCKS):
        un = u[:, n * LANE:(n + 1) * LANE]
        ra.append(jnp.dot(un, wa_ref[n], precision=HI, preferred_element_type=F32))
        rx.append(jnp.dot(un, wx_ref[n], precision=HI, preferred_element_type=F32))
    r = jax.nn.sigmoid(jnp.concatenate(ra, axis=-1) + ba)
    ig = jax.nn.sigmoid(jnp.concatenate(rx, axis=-1) + bx)
    log_a = -RG_C * r * sp
    a = jnp.exp(log_a)
    z = 2.0 * log_a
    u2 = jnp.exp(z)
    em1 = jnp.where(u2 == 1.0, z, (u2 - 1.0) * z / jnp.log(u2))
    mult = jnp.sqrt(-em1)
    return a, mult, ig * u


def _scan_rows(a, b, t, n_steps):
    n = a.shape[0]
    for lv in range(n_steps):
        d = 1 << lv
        ok = t >= d
        a_s = jnp.where(ok, pltpu.roll(a, d, 0), 1.0)
        b_s = jnp.where(ok, pltpu.roll(b, d, 0), 0.0)
        b = a * b_s + b
        a = a * a_s
    return a, b


def _rglru_prompt_kernel(x_ref, gc_ref, cw_ref, cb_ref, wa_ref, wx_ref, ba_ref, bx_ref, lam_ref,
                         obuf_ref, o_ref, hf_ref, prev_ref, h_ref, *, Tc):
    del obuf_ref
    i = pl.program_id(1)

    @pl.when(i == 0)
    def _():
        prev_ref[...] = jnp.zeros(prev_ref.shape, F32)
        h_ref[...] = jnp.zeros(h_ref.shape, F32)

    x = x_ref[...]
    prev = prev_ref[...]
    t8 = lax.broadcasted_iota(jnp.int32, prev.shape, 0)
    cw = cw_ref[...]
    u = cb_ref[...] + x * cw[CONV_W - 1:CONV_W]
    for j in range(CONV_W - 1):
        d = CONV_W - 1 - j
        xs = pltpu.roll(x, d, 0)
        head = jnp.where(t8 >= d, xs[:SUBLANE], pltpu.roll(prev, d, 0))
        u = u + jnp.concatenate([head, xs[SUBLANE:]], axis=0) * cw[j:j + 1]
    prev_ref[...] = x[Tc - SUBLANE:]
    sp = jax.nn.softplus(-lam_ref[...])
    a, mult, iu = _rg_gates(u, wa_ref, wx_ref, ba_ref[...], bx_ref[...], sp)
    t = lax.broadcasted_iota(jnp.int32, a.shape, 0)
    mult = jnp.where((t == 0) & (i == 0), 1.0, mult)
    acc_a, acc_b = _scan_rows(a, mult * iu, t, int(math.log2(Tc)))
    h = acc_a * h_ref[0:1] + acc_b
    h_ref[...] = jnp.broadcast_to(h[Tc - 1:Tc], h_ref.shape)
    o_ref[...] = (h * jax.nn.gelu(gc_ref[...])).astype(BF16)

    @pl.when(i == pl.num_programs(1) - 1)
    def _():
        hf_ref[...] = h[Tc - 1:Tc]


def _rg_weight_specs(W, nidx):
    z = lambda shape: pl.BlockSpec(shape, lambda *_: (0,) * len(shape))
    return [z((CONV_W, W)), z((1, W)), z((C_BLOCKS, LANE, LANE)), z((C_BLOCKS, LANE, LANE)),
            z((1, W)), z((1, W)), z((1, W))]


def rglru_prompt(proj, wts, o_buf, B, S):
    W = C_BLOCKS * LANE
    Tc = _pick(S, (512, 256, 128))
    nt = S // Tc
    xcol = (3 * A_HEADS + 4 * B_HEADS) * LANE // W
    args = (proj, proj, *wts, o_buf)
    return pl.pallas_call(
        functools.partial(_rglru_prompt_kernel, Tc=Tc),
        grid=(B, nt),
        in_specs=[pl.BlockSpec((Tc, W), lambda b, i: (b * nt + i, xcol)),
                  pl.BlockSpec((Tc, W), lambda b, i: (b * nt + i, xcol + 1))] + _rg_weight_specs(W, 2)
                 + [pl.BlockSpec(memory_space=pl.ANY)],
        out_specs=[pl.BlockSpec((Tc, W), lambda b, i: (b * nt + i, 0)),
                   pl.BlockSpec((None, 1, W), lambda b, i: (b, 0, 0))],
        out_shape=[jax.ShapeDtypeStruct(o_buf.shape, BF16), jax.ShapeDtypeStruct((B, 1, W), F32)],
        input_output_aliases={len(args) - 1: 0},
        scratch_shapes=[pltpu.VMEM((SUBLANE, W), F32), pltpu.VMEM((SUBLANE, W), F32)],
        compiler_params=_cparams(("parallel", "arbitrary")),
        name="rglru_prompt",
    )(*args)


def _rglru_sample_kernel(x_ref, gc_ref, p_ref, h0_ref, cw_ref, cb_ref, wa_ref, wx_ref, ba_ref, bx_ref, lam_ref,
                         obuf_ref, o_ref, h_ref, *, Tq):
    del obuf_ref
    x = x_ref[...]
    n = x.shape[0]
    t = lax.broadcasted_iota(jnp.int32, x.shape, 0) % Tq
    p = p_ref[...]
    cw = cw_ref[...]
    u = cb_ref[...] + x * cw[CONV_W - 1:CONV_W]
    for j in range(CONV_W - 1):
        d = CONV_W - 1 - j
        shifted = jnp.where(t >= d, pltpu.roll(x, d, 0), pltpu.roll(p, n - (Tq - d), 0))
        u = u + shifted * cw[j:j + 1]
    sp = jax.nn.softplus(-lam_ref[...])
    a, mult, iu = _rg_gates(u, wa_ref, wx_ref, ba_ref[...], bx_ref[...], sp)
    acc_a, acc_b = _scan_rows(a, mult * iu, t, int(math.log2(Tq)))
    h = acc_a * h0_ref[...] + acc_b
    h_ref[...] = h
    o_ref[...] = (h * jax.nn.gelu(gc_ref[...])).astype(BF16)


def rglru_sample(proj, wts, pbuf, h0rep, o_buf, NP, NS, Tq):
    W = C_BLOCKS * LANE
    tr = _pick(NS, (256, 128, 64, 32, 16, 8))
    xcol = (3 * A_HEADS + 4 * B_HEADS) * LANE // W
    base = NP // tr
    loc = pl.BlockSpec((tr, W), lambda i: (i, 0))
    args = (proj, proj, pbuf, h0rep, *wts, o_buf)
    return pl.pallas_call(
        functools.partial(_rglru_sample_kernel, Tq=Tq),
        grid=(NS // tr,),
        in_specs=[pl.BlockSpec((tr, W), lambda i: (base + i, xcol)),
                  pl.BlockSpec((tr, W), lambda i: (base + i, xcol + 1)), loc, loc] + _rg_weight_specs(W, 1)
                 + [pl.BlockSpec(memory_space=pl.ANY)],
        out_specs=[pl.BlockSpec((tr, W), lambda i: (base + i, 0)), loc],
        out_shape=[jax.ShapeDtypeStruct(o_buf.shape, BF16), jax.ShapeDtypeStruct((NS, W), F32)],
        input_output_aliases={len(args) - 1: 0},
        compiler_params=_cparams(("parallel",)),
        name="rglru_sample",
    )(*args)


def _out_proj_kernel(x_ref, a_ref, b_ref, c_ref, wa_ref, wb_ref, wc_ref, o_ref):
    y = jnp.dot(a_ref[...], wa_ref[...], preferred_element_type=F32)
    y = y + jnp.dot(b_ref[...], wb_ref[...], preferred_element_type=F32)
    y = y + jnp.dot(c_ref[...], wc_ref[...], preferred_element_type=F32)
    o_ref[...] = x_ref[...] + y


def out_proj(x, oa, ob, oc, w):
    n, d = x.shape
    wa_, wb_, wc_ = oa.shape[1], ob.shape[1], oc.shape[1]
    assert wb_ == wc_ and wa_ % wb_ == 0
    tm = _pick(n, (1024, 512, 256, 128))
    tn = _pick(d, (512, 256, 128))
    return pl.pallas_call(
        _out_proj_kernel,
        grid=(n // tm, d // tn),
        in_specs=[pl.BlockSpec((tm, tn), lambda i, j: (i, j)),
                  pl.BlockSpec((tm, wa_), lambda i, j: (i, 0)),
                  pl.BlockSpec((tm, wb_), lambda i, j: (i, 0)),
                  pl.BlockSpec((tm, wc_), lambda i, j: (i, 0)),
                  pl.BlockSpec((wa_, tn), lambda i, j: (0, j)),
                  pl.BlockSpec((wb_, tn), lambda i, j: (wa_ // wb_, j)),
                  pl.BlockSpec((wc_, tn), lambda i, j: (wa_ // wb_ + 1, j))],
        out_specs=pl.BlockSpec((tm, tn), lambda i, j: (i, j)),
        out_shape=jax.ShapeDtypeStruct((n, d), F32),
        compiler_params=_cparams(("parallel", "arbitrary")),
        name="out_proj",
    )(x, oa, ob, oc, w, w, w)


def _ffn_up_kernel(te_ref, nu_ref, x_ref, g_ref, wg_ref, wu_ref, o_ref, h_ref, *, do_norm):
    i = pl.program_id(0)

    @pl.when(pl.program_id(1) == 0)
    def _():
        x = x_ref[...]
        if do_norm:
            x = _rms(x, g_ref[...])
        h_ref[...] = x.astype(BF16)

    @pl.when(i < nu_ref[0])
    def _():
        h = h_ref[...]
        a = jnp.dot(h, wg_ref[...], preferred_element_type=F32)
        b = jnp.dot(h, wu_ref[...], preferred_element_type=F32)
        o_ref[...] = (jax.nn.silu(a) * b).astype(BF16)

    @pl.when(i >= nu_ref[0])
    def _():
        o_ref[...] = jnp.zeros(o_ref.shape, BF16)


def ffn_up(x, g, wg, wu, te, nu, tm, do_norm):
    n, d = x.shape
    f = wg.shape[-1]
    tf = _pick(f, (512, 256, 128))
    grid_spec = pltpu.PrefetchScalarGridSpec(
        num_scalar_prefetch=2,
        grid=(n // tm, f // tf),
        in_specs=[pl.BlockSpec((tm, d), lambda i, j, te, nu: (i, 0)),
                  pl.BlockSpec((1, d), lambda i, j, te, nu: (0, 0)),
                  pl.BlockSpec((None, d, tf), lambda i, j, te, nu: (te[i], 0, j)),
                  pl.BlockSpec((None, d, tf), lambda i, j, te, nu: (te[i], 0, j))],
        out_specs=pl.BlockSpec((tm, tf), lambda i, j, te, nu: (i, j)),
        scratch_shapes=[pltpu.VMEM((tm, d), BF16)])
    return pl.pallas_call(
        functools.partial(_ffn_up_kernel, do_norm=do_norm),
        grid_spec=grid_spec,
        out_shape=jax.ShapeDtypeStruct((n, f), BF16),
        compiler_params=_cparams(("parallel", "arbitrary")),
        name="ffn_up",
    )(te, nu, x, g.reshape(1, d), wg, wu)


def _ffn_down_kernel(te_ref, nu_ref, a_ref, wd_ref, *rest, residual):
    i = pl.program_id(0)
    if residual:
        r_ref, o_ref = rest
    else:
        (o_ref,) = rest

    @pl.when(i < nu_ref[0])
    def _():
        y = jnp.dot(a_ref[...], wd_ref[...], preferred_element_type=F32)
        if residual:
            y = r_ref[...] + y
        o_ref[...] = y

    @pl.when(i >= nu_ref[0])
    def _():
        o_ref[...] = jnp.zeros(o_ref.shape, F32)


def ffn_down(act, wd, te, nu, tm, res):
    n, f = act.shape
    d = wd.shape[-1]
    tn = _pick(d, (512, 256, 128))
    in_specs = [pl.BlockSpec((tm, f), lambda i, j, te, nu: (i, 0)),
                pl.BlockSpec((None, f, tn), lambda i, j, te, nu: (te[i], 0, j))]
    args = [act, wd]
    if res is not None:
        in_specs.append(pl.BlockSpec((tm, tn), lambda i, j, te, nu: (i, j)))
        args.append(res)
    grid_spec = pltpu.PrefetchScalarGridSpec(
        num_scalar_prefetch=2,
        grid=(n // tm, d // tn),
        in_specs=in_specs,
        out_specs=pl.BlockSpec((tm, tn), lambda i, j, te, nu: (i, j)))
    return pl.pallas_call(
        functools.partial(_ffn_down_kernel, residual=res is not None),
        grid_spec=grid_spec,
        out_shape=jax.ShapeDtypeStruct((n, d), F32),
        compiler_params=_cparams(("parallel", "arbitrary")),
        name="ffn_down",
    )(te, nu, *args)


def _router_kernel(x_ref, g_ref, wr_ref, h_ref, rt_ref):
    h = _rms(x_ref[...], g_ref[...])
    h_ref[...] = h
    logits = jnp.dot(h, wr_ref[...], precision=HI, preferred_element_type=F32)
    lane = lax.broadcasted_iota(jnp.int32, logits.shape, 1)
    l1 = jnp.where(lane < N_EXPERTS, logits, -jnp.inf)
    m1 = jnp.max(l1, axis=-1, keepdims=True)
    i1 = jnp.min(jnp.where(l1 == m1, lane, LANE), axis=-1, keepdims=True)
    l2 = jnp.where(lane == i1, -jnp.inf, l1)
    m2 = jnp.max(l2, axis=-1, keepdims=True)
    i2 = jnp.min(jnp.where(l2 == m2, lane, LANE), axis=-1, keepdims=True)
    e = jnp.exp(m2 - m1)
    g1 = 1.0 / (1.0 + e)
    g2 = e / (1.0 + e)
    rt_ref[...] = jnp.where(lane == 0, i1.astype(F32),
                            jnp.where(lane == 1, i2.astype(F32),
                                      jnp.where(lane == 2, g1, jnp.where(lane == 3, g2, 0.0))))


def route_tokens(x, g, wr):
    n, d = x.shape
    tm = _pick(n, (512, 256, 128))
    wr_pad = jnp.zeros((d, LANE), F32).at[:, :N_EXPERTS].set(wr)
    return pl.pallas_call(
        _router_kernel,
        grid=(n // tm,),
        in_specs=[pl.BlockSpec((tm, d), lambda i: (i, 0)),
                  pl.BlockSpec((1, d), lambda i: (0, 0)),
                  pl.BlockSpec((d, LANE), lambda i: (0, 0))],
        out_specs=[pl.BlockSpec((tm, d), lambda i: (i, 0)), pl.BlockSpec((tm, LANE), lambda i: (i, 0))],
        out_shape=[jax.ShapeDtypeStruct((n, d), F32), jax.ShapeDtypeStruct((n, LANE), F32)],
        compiler_params=_cparams(("parallel",)),
        name="router",
    )(x, g.reshape(1, d), wr_pad)


DISPATCH_ROWS = 256
COMBINE_ROWS = 128
GATHER_UNROLL = 8


def _gather_start(idx_ref, idx0, src_ref, dst_ref, sem, rows):
    def issue(t, c):
        pltpu.make_async_copy(src_ref.at[pl.ds(idx_ref[idx0 + t], 1)], dst_ref.at[pl.ds(t, 1)], sem).start()
        return c
    lax.fori_loop(0, rows, issue, 0, unroll=GATHER_UNROLL)


def _gather_wait(src_ref, dst_ref, sem, rows):
    pltpu.make_async_copy(src_ref.at[pl.ds(0, rows)], dst_ref, sem).wait()


def _dispatch_kernel(src_ref, nrows_ref, h_ref, o_ref, buf_ref, sem):
    i = pl.program_id(0)
    R = DISPATCH_ROWS
    live = lambda s: s * R < nrows_ref[0]

    def start(s):
        _gather_start(src_ref, s * R, h_ref, buf_ref.at[s % 2], sem.at[s % 2], R)

    @pl.when(i == 0)
    def _():
        start(i)

    @pl.when((i + 1 < pl.num_programs(0)) & live(i + 1))
    def _():
        start(i + 1)

    @pl.when(live(i))
    def _():
        _gather_wait(h_ref, buf_ref.at[i % 2], sem.at[i % 2], R)
        o_ref[...] = buf_ref[i % 2].astype(BF16)

    @pl.when(jnp.logical_not(live(i)))
    def _():
        o_ref[...] = jnp.zeros(o_ref.shape, BF16)


def dispatch(h, src, nrows):
    n, d = h.shape
    p_rows = src.shape[0]
    assert p_rows % DISPATCH_ROWS == 0
    grid_spec = pltpu.PrefetchScalarGridSpec(
        num_scalar_prefetch=2,
        grid=(p_rows // DISPATCH_ROWS,),
        in_specs=[pl.BlockSpec(memory_space=pl.ANY)],
        out_specs=pl.BlockSpec((DISPATCH_ROWS, d), lambda i, src, nr: (i, 0)),
        scratch_shapes=[pltpu.VMEM((2, DISPATCH_ROWS, d), F32), pltpu.SemaphoreType.DMA((2,))])
    return pl.pallas_call(
        _dispatch_kernel,
        grid_spec=grid_spec,
        out_shape=jax.ShapeDtypeStruct((p_rows, d), BF16),
        compiler_params=_cparams(("arbitrary",)),
        name="moe_dispatch",
    )(src, nrows, h)


def _combine_kernel(pos_ref, x_ref, rt_ref, y_ref, o_ref, buf_ref, sem, *, n):
    i = pl.program_id(0)
    R = COMBINE_ROWS

    def start(s):
        for k in range(TOP_K):
            _gather_start(pos_ref, k * n + s * R, y_ref, buf_ref.at[s % 2, k], sem.at[s % 2], R)

    @pl.when(i == 0)
    def _():
        start(i)

    @pl.when(i + 1 < pl.num_programs(0))
    def _():
        start(i + 1)

    for k in range(TOP_K):
        _gather_wait(y_ref, buf_ref.at[i % 2, k], sem.at[i % 2], R)
    rt = rt_ref[...]
    o_ref[...] = x_ref[...] + rt[:, 2:3] * buf_ref[i % 2, 0] + rt[:, 3:4] * buf_ref[i % 2, 1]


def combine(x, rt, y, pos):
    n, d = x.shape
    assert n % COMBINE_ROWS == 0
    grid_spec = pltpu.PrefetchScalarGridSpec(
        num_scalar_prefetch=1,
        grid=(n // COMBINE_ROWS,),
        in_specs=[pl.BlockSpec((COMBINE_ROWS, d), lambda i, pos: (i, 0)),
                  pl.BlockSpec((COMBINE_ROWS, LANE), lambda i, pos: (i, 0)),
                  pl.BlockSpec(memory_space=pl.ANY)],
        out_specs=pl.BlockSpec((COMBINE_ROWS, d), lambda i, pos: (i, 0)),
        scratch_shapes=[pltpu.VMEM((2, TOP_K, COMBINE_ROWS, d), F32), pltpu.SemaphoreType.DMA((2,))])
    return pl.pallas_call(
        functools.partial(_combine_kernel, n=n),
        grid_spec=grid_spec,
        out_shape=jax.ShapeDtypeStruct((n, d), F32),
        compiler_params=_cparams(("arbitrary",)),
        name="moe_combine",
    )(pos, x, rt, y)


def _route_plan(rt, tm, n_tiles):
    n = rt.shape[0]
    e_flat = jnp.concatenate([rt[:, 0], rt[:, 1]]).astype(jnp.int32)
    onehot = (e_flat[:, None] == jnp.arange(N_EXPERTS)[None, :]).astype(jnp.int32)
    csum = jnp.cumsum(onehot, axis=0)
    rank = jnp.sum(csum * onehot, axis=1) - 1
    counts = csum[-1]
    padded = ((counts + tm - 1) // tm) * tm
    gend = jnp.cumsum(padded)
    gstart = gend - padded
    pos = jnp.sum(onehot * gstart[None, :], axis=1) + rank
    nu = (gend[-1] // tm).astype(jnp.int32).reshape(1)
    tile_start = jnp.arange(n_tiles, dtype=jnp.int32) * tm
    te = jnp.sum((tile_start[:, None] >= gend[None, :]).astype(jnp.int32), axis=1)
    last = jnp.max(jnp.where(counts > 0, jnp.arange(N_EXPERTS), 0))
    te = jnp.minimum(te, last).astype(jnp.int32)
    pos = pos.astype(jnp.int32)
    tok = jnp.arange(TOP_K * n, dtype=jnp.int32) % n
    src = jnp.zeros((n_tiles * tm,), jnp.int32).at[pos].set(tok)
    return pos, src, te, nu


def ffn_moe(x, g, wr, wg, wu, wd):
    n, d = x.shape
    tm = _pick(n, (512, 256, 128))
    n_tiles = (TOP_K * n) // tm + N_EXPERTS
    h, rt = route_tokens(x, g, wr)
    pos, src, te, nu = _route_plan(rt, tm, n_tiles)
    xs = dispatch(h, src, nu * tm)
    act = ffn_up(xs, g, wg, wu, te, nu, tm, do_norm=False)
    y = ffn_down(act, wd, te, nu, tm, None)
    return combine(x, rt, y, pos)


def ffn_dense(x, g, wg, wu, wd):
    n, d = x.shape
    tm = _pick(n, (1024, 512, 256, 128))
    te = jnp.zeros((n // tm,), jnp.int32)
    nu = jnp.full((1,), n // tm, jnp.int32)
    act = ffn_up(x, g, wg[None], wu[None], te, nu, tm, do_norm=True)
    return ffn_down(act, wd[None], te, nu, tm, x)


def kernel(x_prompt, x_sample, cache_k, cache_v, page_table, state_hgrn, state_rglru, state_conv,
           w_in, w_out, g_mix, g_ffn, q_norm, k_norm, lam_q1, lam_k1, lam_q2, lam_k2, subln,
           rel_table, hgrn_lb, hgrn_onorm, conv_w, conv_b, rg_wa, rg_ba, rg_wx, rg_bx, rg_lambda,
           ff_gate, ff_up, ff_down, router, ex_gate, ex_up, ex_down):
    B, S, D = x_prompt.shape
    DB, Tq, _ = x_sample.shape
    depth = w_in.shape[0]
    NP, NS = B * S, DB * Tq
    AW = A_HEADS * LANE
    CW = C_BLOCKS * LANE
    xc0 = 3 * AW + 4 * B_HEADS * LANE
    x = jnp.concatenate([x_prompt.reshape(NP, D), x_sample.reshape(NS, D)], axis=0)

    sm = jax.nn.softmax(hgrn_lb.astype(F32), axis=0)
    lb_all = jnp.cumsum(sm, axis=0) - sm[0:1]

    outs = {k: [] for k in ("kp", "vp", "sp", "hp", "bp", "ks", "vs", "ss", "hs", "bs")}
    for l in range(depth):
        lam_init = 0.8 - 0.6 * math.exp(-0.3 * l)
        lamp = jnp.stack([lam_q1[l], lam_k1[l], lam_q2[l], lam_k2[l]]).astype(F32)
        proj = norm_matmul(x, g_mix[l], w_in[l].astype(BF16))
        ta = _pick(math.gcd(S, NP + NS), (512, 256, 128))
        qf, kf, kb, qt, vt = qk_prep(proj, q_norm[l], k_norm[l], ta)
        vf = proj[:, 2 * AW:3 * AW]
        oa = attn_prompt(qt, kb, vt, rel_table, lamp, subln[l], jnp.zeros((NP + NS, AW), BF16), B, S, ta, lam_init)
        kn2 = kf[NP:].reshape(DB, Tq * A_HEADS, LANE)
        vn2 = vf[NP:].reshape(DB, Tq * A_HEADS, LANE)
        oa = attn_sample(qf, kn2, vn2, cache_k, cache_v, l, page_table, rel_table, lamp, subln[l], oa,
                         NP, DB, Tq, lam_init)
        ob, s_p = hgrn(proj, lb_all[l], hgrn_onorm[l], 0, B, S, None, l, jnp.zeros((NP + NS, B_HEADS * LANE), BF16))
        ob, s_s = hgrn(proj, lb_all[l], hgrn_onorm[l], NP, DB, Tq, state_hgrn, l, ob)
        wts = (conv_w[l], conv_b[l].reshape(1, CW), rg_wa[l], rg_wx[l], rg_ba[l].reshape(1, CW),
               rg_bx[l].reshape(1, CW), rg_lambda[l].reshape(1, CW))
        oc, h_p = rglru_prompt(proj, wts, jnp.zeros((NP + NS, CW), BF16), B, S)
        pbuf = jnp.concatenate([jnp.zeros((DB, Tq - (CONV_W - 1), CW), F32), state_conv[l]], axis=1).reshape(NS, CW)
        h0rep = jnp.repeat(state_rglru[l], Tq, axis=0)
        oc, h_s = rglru_sample(proj, wts, pbuf, h0rep, oc, NP, NS, Tq)
        x = out_proj(x, oa, ob, oc, w_out[l].astype(BF16))
        m = l // 2
        if l % 2 == 0:
            x = ffn_dense(x, g_ffn[l], ff_gate[m].astype(BF16), ff_up[m].astype(BF16), ff_down[m].astype(BF16))
        else:
            x = ffn_moe(x, g_ffn[l], router[m], ex_gate[m].astype(BF16), ex_up[m].astype(BF16),
                        ex_down[m].astype(BF16))
        xcs = proj[:, xc0:xc0 + CW]
        outs["kp"].append(kf[:NP].reshape(B, S, A_HEADS, LANE))
        outs["vp"].append(vf[:NP].reshape(B, S, A_HEADS, LANE))
        outs["sp"].append(s_p)
        outs["hp"].append(h_p.reshape(B, CW))
        outs["bp"].append(xcs[:NP].reshape(B, S, CW)[:, S - (CONV_W - 1):])
        outs["ks"].append(kf[NP:].reshape(DB, Tq, A_HEADS, LANE))
        outs["vs"].append(vf[NP:].reshape(DB, Tq, A_HEADS, LANE))
        outs["ss"].append(s_s)
        outs["hs"].append(h_s.reshape(DB, Tq, CW)[:, Tq - 1])
        outs["bs"].append(xcs[NP:].reshape(DB, Tq, CW)[:, Tq - (CONV_W - 1):])
    st = {k: jnp.stack(v) for k, v in outs.items()}
    return (x[:NP].reshape(B, S, D), x[NP:].reshape(DB, Tq, D),
            st["kp"], st["vp"], st["sp"], st["hp"], st["bp"],
            st["ks"], st["vs"], st["ss"], st["hs"], st["bs"])
```

```python
import functools
import math

import jax
import jax.numpy as jnp
from jax import lax
from jax.experimental import pallas as pl
from jax.experimental.pallas import tpu as pltpu

F32 = jnp.float32
BF16 = jnp.bfloat16
HI = lax.Precision.HIGHEST
EPS = 1e-6
NEG = -1e30
LANE = 128
SUBLANE = 8
NT = (((1,), (1,)), ((), ()))
TN = (((0,), (0,)), ((), ()))

A_HEADS = 8
A_SUB = 64
A_VDIM = 128
A_SCALE = A_SUB ** -0.5
LOG2E = math.log2(math.e)
Q_SCALE = A_SCALE * LOG2E
B_HEADS = 4
C_BLOCKS = 4
CONV_W = 4
RG_C = 8.0
PAGE = 128
N_BUCKETS = 32
MAX_EXACT = 16
MAX_DISTANCE = 128
N_EXPERTS = 8
TOP_K = 2


def _cparams(sem, vmem_mb=48):
    return pltpu.CompilerParams(dimension_semantics=sem, vmem_limit_bytes=vmem_mb << 20)


def _pick(n, prefs):
    for p in prefs:
        if n % p == 0:
            return p
    raise ValueError(f"no tile in {prefs} divides {n}")


def _rms(x, g):
    return x * lax.rsqrt(jnp.mean(x * x, axis=-1, keepdims=True) + EPS) * g


def _norm_matmul_kernel(x_ref, g_ref, w_ref, o_ref, h_ref):
    @pl.when(pl.program_id(1) == 0)
    def _():
        h_ref[...] = _rms(x_ref[...], g_ref[...]).astype(BF16)

    o_ref[...] = jnp.dot(h_ref[...], w_ref[...], preferred_element_type=F32)


def norm_matmul(x, g, w):
    n, d = x.shape
    c = w.shape[1]
    tm = _pick(n, (1024, 512, 256, 128))
    tn = _pick(c, (1024, 512, 256, 128))
    return pl.pallas_call(
        _norm_matmul_kernel,
        grid=(n // tm, c // tn),
        in_specs=[pl.BlockSpec((tm, d), lambda i, j: (i, 0)),
                  pl.BlockSpec((1, d), lambda i, j: (0, 0)),
                  pl.BlockSpec((d, tn), lambda i, j: (0, j))],
        out_specs=pl.BlockSpec((tm, tn), lambda i, j: (i, j)),
        out_shape=jax.ShapeDtypeStruct((n, c), F32),
        scratch_shapes=[pltpu.VMEM((tm, d), BF16)],
        compiler_params=_cparams(("parallel", "arbitrary")),
        name="norm_matmul",
    )(x, g.reshape(1, d), w)


def _qk_prep_kernel(q_ref, k_ref, v_ref, qg_ref, kg_ref, qf_ref, kf_ref, kb_ref, qt_ref, vt_ref):
    r = lax.broadcasted_iota(jnp.int32, (LANE, LANE), 0)
    c = lax.broadcasted_iota(jnp.int32, (LANE, LANE), 1)
    seg = jnp.where((r // A_SUB) == (c // A_SUB), 1.0 / A_SUB, 0.0).astype(F32)

    def normed(src, g_ref, sl):
        x = src[:, sl]
        ms = jnp.dot(x * x, seg, precision=HI, preferred_element_type=F32)
        return x * lax.rsqrt(ms + EPS) * g_ref[...]

    for h in range(A_HEADS):
        sl = slice(h * LANE, (h + 1) * LANE)
        q = normed(q_ref, qg_ref, sl) * Q_SCALE
        qf_ref[:, sl] = q
        qt_ref[h] = q.T.astype(BF16)
        k = normed(k_ref, kg_ref, sl)
        kf_ref[:, sl] = k
        kb_ref[:, sl] = k.astype(BF16)
        vt_ref[h] = v_ref[:, sl].T.astype(BF16)


def qk_prep(proj, q_norm, k_norm, tm):
    n = proj.shape[0]
    w = A_HEADS * LANE
    qg = jnp.concatenate([q_norm, q_norm]).reshape(1, LANE)
    kg = jnp.concatenate([k_norm, k_norm]).reshape(1, LANE)
    col = lambda cb: pl.BlockSpec((tm, w), lambda i, cb=cb: (i, cb))
    vec = pl.BlockSpec((1, LANE), lambda i: (0, 0))
    out = pl.BlockSpec((tm, w), lambda i: (i, 0))
    out_t = pl.BlockSpec((A_HEADS, None, LANE, tm), lambda i: (0, i, 0, 0))
    t_shape = jax.ShapeDtypeStruct((A_HEADS, n // tm, LANE, tm), BF16)
    return pl.pallas_call(
        _qk_prep_kernel,
        grid=(n // tm,),
        in_specs=[col(0), col(1), col(2), vec, vec],
        out_specs=[out, out, out, out_t, out_t],
        out_shape=[jax.ShapeDtypeStruct((n, w), F32)] * 2 + [jax.ShapeDtypeStruct((n, w), BF16), t_shape, t_shape],
        compiler_params=_cparams(("parallel",)),
        name="qk_prep",
    )(proj, proj, proj, qg, kg)


def _rel_bias(table, n):
    nf = jnp.maximum(n, 1).astype(F32)
    far = MAX_EXACT + (jnp.log(nf / MAX_EXACT) / math.log(MAX_DISTANCE / MAX_EXACT)
                       * (N_BUCKETS - MAX_EXACT)).astype(jnp.int32)
    bucket = jnp.where(n < MAX_EXACT, n, jnp.minimum(far, N_BUCKETS - 1))
    onehot = (bucket.reshape(1, -1) == jnp.arange(N_BUCKETS)[:, None]).astype(F32)
    t = table.astype(F32)
    vals = jnp.dot(t.T, onehot, precision=HI)
    return ((vals - t[N_BUCKETS - 1][:, None]) * LOG2E).reshape((t.shape[1],) + n.shape)


def _lam_from(lamp, lam_init):
    s1 = jnp.sum(lamp[0:1] * lamp[1:2], axis=-1, keepdims=True)
    s2 = jnp.sum(lamp[2:3] * lamp[3:4], axis=-1, keepdims=True)
    return jnp.exp(s1) - jnp.exp(s2) + lam_init


def _attn_prompt_kernel(q_ref, k_ref, v_ref, bias_ref, lamp_ref, subln_ref, obuf_ref, o_ref,
                        m_ref, l_ref, acc_ref, *, T, lam_init):
    del obuf_ref
    qi = pl.program_id(2)
    qt = q_ref[...]
    sub = lax.broadcasted_iota(jnp.int32, qt.shape, 0)
    zero = jnp.zeros_like(qt)
    qs = (jnp.where(sub < A_SUB, qt, zero), jnp.where(sub >= A_SUB, qt, zero))
    m_ref[...] = jnp.full(m_ref.shape, NEG, F32)
    l_ref[...] = jnp.zeros(l_ref.shape, F32)
    acc_ref[...] = jnp.zeros(acc_ref.shape, F32)

    def tile(kt, bias):
        k = k_ref[pl.ds(pl.multiple_of(kt * T, T), T), :]
        vt = v_ref[kt]
        for c in range(2):
            s = jnp.dot(k, qs[c], preferred_element_type=F32)
            if bias is not None:
                s = s + bias
            m_old = m_ref[c]
            m_new = jnp.maximum(m_old, jnp.max(s, axis=0, keepdims=True))
            alpha = jnp.exp2(m_old - m_new)
            p = jnp.exp2(s - m_new)
            l_ref[c] = alpha * l_ref[c] + jnp.sum(p, axis=0, keepdims=True)
            acc_ref[c] = alpha * acc_ref[c] + jnp.dot(vt, p.astype(BF16), preferred_element_type=F32)
            m_ref[c] = m_new

    n_far = jnp.maximum(qi - 1, 0)

    def far_pair(pr, carry):
        tile(2 * pr, None)
        tile(2 * pr + 1, None)
        return carry

    lax.fori_loop(0, n_far // 2, far_pair, 0)

    @pl.when(n_far % 2 == 1)
    def _():
        tile(n_far - 1, None)

    nb = T // LANE
    same, after = bias_ref[0], bias_ref[1]
    zero_b = jnp.zeros((LANE, LANE), F32)
    neg_b = jnp.full((LANE, LANE), NEG, F32)
    grid_of = lambda pick: jnp.concatenate(
        [jnp.concatenate([pick(jk, iq) for iq in range(nb)], axis=1) for jk in range(nb)], axis=0)

    @pl.when(qi >= 1)
    def _():
        tile(qi - 1, grid_of(lambda jk, iq: after if (jk == nb - 1 and iq == 0) else zero_b))

    tile(qi, grid_of(lambda jk, iq: neg_b if iq < jk else same if iq == jk else after if iq == jk + 1 else zero_b))

    lam = _lam_from(lamp_ref[...], lam_init)
    ot = acc_ref[0] / l_ref[0] - lam * (acc_ref[1] / l_ref[1])
    ms = jnp.mean(ot * ot, axis=0, keepdims=True)
    y = ot * lax.rsqrt(ms + EPS) * subln_ref[...] * (1.0 - lam_init)
    o_ref[...] = y.T.astype(BF16)


def attn_prompt(qt, kb, vt, rel_table, lamp, subln, o_buf, B, S, T, lam_init):
    assert T >= MAX_DISTANCE and S % T == 0
    nq = S // T
    assert LANE == MAX_DISTANCE
    d = jnp.arange(LANE)[None, :] - jnp.arange(LANE)[:, None]
    bias = jnp.stack([jnp.where(d >= 0, _rel_bias(rel_table, jnp.maximum(d, 0)), NEG),
                      _rel_bias(rel_table, d + LANE)], axis=1)
    return pl.pallas_call(
        functools.partial(_attn_prompt_kernel, T=T, lam_init=lam_init),
        grid=(B, A_HEADS, nq),
        in_specs=[pl.BlockSpec((None, None, LANE, T), lambda b, h, i: (h, b * nq + i, 0, 0)),
                  pl.BlockSpec((S, LANE), lambda b, h, i: (b, h)),
                  pl.BlockSpec((None, nq, LANE, T), lambda b, h, i: (h, b, 0, 0)),
                  pl.BlockSpec((None, 2, LANE, LANE), lambda b, h, i: (h, 0, 0, 0)),
                  pl.BlockSpec((4, A_SUB), lambda b, h, i: (0, 0)),
                  pl.BlockSpec((LANE, 1), lambda b, h, i: (0, 0)),
                  pl.BlockSpec(memory_space=pl.ANY)],
        out_specs=pl.BlockSpec((T, LANE), lambda b, h, i: (b * nq + i, h)),
        out_shape=jax.ShapeDtypeStruct(o_buf.shape, BF16),
        input_output_aliases={6: 0},
        scratch_shapes=[pltpu.VMEM((2, 1, T), F32), pltpu.VMEM((2, 1, T), F32),
                        pltpu.VMEM((2, LANE, T), F32)],
        compiler_params=_cparams(("parallel", "parallel", "arbitrary")),
        name="attn_prompt",
    )(qt, kb, vt, bias, lamp, subln.reshape(LANE, 1), o_buf)


def _attn_sample_kernel(pt_ref, q_ref, kn_ref, vn_ref, mask_ref, bnew_ref, lamp_ref, subln_ref, *rest,
                        PPS, n_steps, lam_init):
    k_refs = rest[:PPS]
    v_refs = rest[PPS:2 * PPS]
    o_ref = rest[2 * PPS + 1]
    qm_ref, m_ref, l_ref, acc_ref, s_ref = rest[2 * PPS + 2:]
    L = PAGE * A_HEADS
    step = pl.program_id(1)
    Tq = q_ref.shape[0]

    @pl.when(step == 0)
    def _():
        q = q_ref[...]
        lane = lax.broadcasted_iota(jnp.int32, (Tq, LANE), 1)
        rows = []
        for h in range(A_HEADS):
            qh = q[:, h * LANE:(h + 1) * LANE]
            rows.append(jnp.where(lane < A_SUB, qh, 0.0))
            rows.append(jnp.where(lane >= A_SUB, qh, 0.0))
        qm_ref[...] = jnp.concatenate(rows, axis=0).astype(BF16)
        m_ref[...] = jnp.full(m_ref.shape, NEG, F32)
        l_ref[...] = jnp.zeros(l_ref.shape, F32)
        acc_ref[...] = jnp.zeros(acc_ref.shape, F32)

    qm = qm_ref[...]
    mel = None
    for j in range(PPS):
        k2 = k_refs[j][...].reshape(L, LANE).astype(BF16)
        s = lax.dot_general(qm, k2, NT, preferred_element_type=F32)
        if j == PPS - 1:
            s = s + mask_ref[jnp.where(step == n_steps - 1, 1, 0)]
        else:
            s = s + mask_ref[0]
        s_ref[:, j * L:(j + 1) * L] = s
        mel = s if mel is None else jnp.maximum(mel, s)
    m_old = m_ref[...]
    m_new = jnp.maximum(m_old, jnp.max(mel, axis=-1, keepdims=True))
    alpha = jnp.exp2(m_old - m_new)
    lsum = None
    acc = alpha * acc_ref[...]
    for j in range(PPS):
        p = jnp.exp2(s_ref[:, j * L:(j + 1) * L] - m_new)
        lsum = p if lsum is None else lsum + p
        v2 = v_refs[j][...].reshape(L, LANE).astype(BF16)
        acc = acc + jnp.dot(p.astype(BF16), v2, preferred_element_type=F32)
    l_ref[...] = alpha * l_ref[...] + jnp.sum(lsum, axis=-1, keepdims=True)
    acc_ref[...] = acc
    m_ref[...] = m_new

    @pl.when(step == n_steps - 1)
    def _():
        kn2 = kn_ref[...].astype(BF16)
        vn2 = vn_ref[...].astype(BF16)
        s = lax.dot_general(qm, kn2, NT, preferred_element_type=F32) + bnew_ref[...]
        m_fin = jnp.maximum(m_new, jnp.max(s, axis=-1, keepdims=True))
        a_fin = jnp.exp2(m_new - m_fin)
        p = jnp.exp2(s - m_fin)
        l_ref[...] = a_fin * l_ref[...] + jnp.sum(p, axis=-1, keepdims=True)
        acc_ref[...] = a_fin * acc_ref[...] + jnp.dot(p.astype(BF16), vn2, preferred_element_type=F32)
        lam = _lam_from(lamp_ref[...], lam_init)
        o = acc_ref[...] / l_ref[...]
        for h in range(A_HEADS):
            r0 = h * 2 * Tq
            oh = o[r0:r0 + Tq] - lam * o[r0 + Tq:r0 + 2 * Tq]
            o_ref[:, h * LANE:(h + 1) * LANE] = (_rms(oh, subln_ref[...]) * (1.0 - lam_init)).astype(BF16)


def attn_sample(qf, kn2, vn2, cache_k, cache_v, layer, page_table, rel_table, lamp, subln, o_buf,
                NP, DB, Tq, lam_init):
    n_pages = page_table.shape[1]
    PPS = _pick(n_pages, (16, 8, 4, 2, 1))
    n_steps = n_pages // PPS
    assert Tq <= PAGE and PAGE >= MAX_DISTANCE
    R = A_HEADS * 2 * Tq
    L = PAGE * A_HEADS
    t_ = jnp.arange(Tq)[:, None]
    b_last = _rel_bias(rel_table, PAGE + t_ - jnp.arange(PAGE)[None, :])
    d_new = t_ - jnp.arange(Tq)[None, :]
    b_new = jnp.where(d_new >= 0, _rel_bias(rel_table, jnp.maximum(d_new, 0)), NEG)
    same = (jnp.arange(A_HEADS)[:, None] == jnp.arange(A_HEADS)[None, :])[:, None, None, None, :]
    expand = lambda b, nk: jnp.where(same, jnp.broadcast_to(b[:, None, :, :, None], (A_HEADS, 2, Tq, nk, A_HEADS)),
                                     NEG).reshape(R, nk * A_HEADS)
    masks = jnp.stack([expand(jnp.zeros((A_HEADS, Tq, PAGE), F32), PAGE), expand(b_last, PAGE)])
    bnew = expand(b_new, Tq)

    pt_flat = page_table.reshape(-1).astype(jnp.int32)
    page_spec = lambda j: pl.BlockSpec(
        (None, None, PAGE, A_HEADS, LANE),
        lambda b, s, pt, j=j: (layer, pt[b * n_pages + s * PPS + j], 0, 0, 0))
    const2 = lambda shape: pl.BlockSpec(shape, lambda b, s, pt: (0,) * len(shape))
    in_specs = ([pl.BlockSpec((Tq, A_HEADS * LANE), lambda b, s, pt: (NP // Tq + b, 0)),
                 pl.BlockSpec((None, Tq * A_HEADS, LANE), lambda b, s, pt: (b, 0, 0)),
                 pl.BlockSpec((None, Tq * A_HEADS, LANE), lambda b, s, pt: (b, 0, 0)),
                 const2((2, R, L)), const2((R, Tq * A_HEADS)), const2((4, A_SUB)), const2((1, LANE))]
                + [page_spec(j) for j in range(PPS)] * 2 + [pl.BlockSpec(memory_space=pl.ANY)])
    grid_spec = pltpu.PrefetchScalarGridSpec(
        num_scalar_prefetch=1,
        grid=(DB, n_steps),
        in_specs=in_specs,
        out_specs=pl.BlockSpec((Tq, A_HEADS * LANE), lambda b, s, pt: (NP // Tq + b, 0)),
        scratch_shapes=[pltpu.VMEM((R, LANE), BF16), pltpu.VMEM((R, 1), F32), pltpu.VMEM((R, 1), F32),
                        pltpu.VMEM((R, LANE), F32), pltpu.VMEM((R, PPS * L), F32)])
    return pl.pallas_call(
        functools.partial(_attn_sample_kernel, PPS=PPS, n_steps=n_steps, lam_init=lam_init),
        grid_spec=grid_spec,
        out_shape=jax.ShapeDtypeStruct(o_buf.shape, BF16),
        input_output_aliases={8 + 2 * PPS: 0},
        compiler_params=_cparams(("parallel", "arbitrary"), vmem_mb=56),
        name="attn_sample",
    )(pt_flat, qf, kn2, vn2, masks, bnew, lamp, subln.reshape(1, LANE),
      *([cache_k] * PPS), *([cache_v] * PPS), o_buf)


HG_C = 128


def _hgrn_masks(levels, R):
    row = jnp.arange(R)[:, None]
    col = jnp.arange(HG_C)[None, :]
    blocks = [row >= col]
    for lv in range(levels):
        s = 1 << lv
        mid = (row // (2 * s)) * (2 * s) + (s - 1)
        second = (row % (2 * s)) >= s
        blocks.append(jnp.where(second, (col > mid) & (col <= row), (col > row) & (col <= mid)))
    return jnp.concatenate(blocks, axis=0).astype(BF16)


def _hgrn_kernel(*refs, R, levels, has_state):
    if has_state:
        q_ref, f_ref, i_ref, g_ref, lb_ref, on_ref, mk_ref, s0_ref, _, o_ref, sf_ref, S_ref = refs
    else:
        q_ref, f_ref, i_ref, g_ref, lb_ref, on_ref, mk_ref, _, o_ref, sf_ref, S_ref = refs
    c = pl.program_id(1)

    @pl.when(c == 0)
    def _():
        S_ref[...] = s0_ref[...] if has_state else jnp.zeros(S_ref.shape, F32)

    row = lax.broadcasted_iota(jnp.int32, (R, HG_C), 0)
    col = lax.broadcasted_iota(jnp.int32, (R, HG_C), 1)
    eye = (lax.broadcasted_iota(jnp.int32, (LANE, LANE), 0) == lax.broadcasted_iota(jnp.int32, (LANE, LANE), 1))
    mk = mk_ref[...]
    dot = functools.partial(jnp.dot, preferred_element_type=F32)

    def pad_bf16(x):
        if R < HG_C:
            x = jnp.concatenate([x, jnp.zeros((HG_C - R, x.shape[1]), F32)], axis=0)
        return x.astype(BF16)

    for h in range(B_HEADS):
        sl = slice(h * LANE, (h + 1) * LANE)
        lb = lb_ref[:, sl]
        f = lb + (1.0 - lb) * jax.nn.sigmoid(f_ref[:, sl])
        g = jnp.log(f)
        kk = 1.0 - f
        qq = jax.nn.silu(q_ref[:, sl])
        v = pad_bf16(i_ref[:, sl])
        g0 = g.astype(BF16).astype(F32)
        r1 = g - g0
        g1 = r1.astype(BF16).astype(F32)
        e3 = dot(mk, pad_bf16(jnp.concatenate([g0, g1, r1 - g1], axis=-1)))
        e = e3[:, :LANE] + e3[:, LANE:2 * LANE] + e3[:, 2 * LANE:]
        b = e[:R]
        A = jnp.where(row == col, jnp.sum(qq * kk, axis=-1, keepdims=True), 0.0)
        for lv in range(levels):
            s = 1 << lv
            x = jnp.exp(e[(lv + 1) * R:(lv + 2) * R])
            second = (row % (2 * s)) >= s
            ql = jnp.where(second, qq * x, 0.0).astype(BF16)
            kl = pad_bf16(jnp.where(second, 0.0, kk * x))
            al = lax.dot_general(ql, kl, NT, preferred_element_type=F32)
            A = A + jnp.where((row // (2 * s)) == (col // (2 * s)), al, 0.0)
        S = S_ref[h]
        o = dot((qq * jnp.exp(b)).astype(BF16), S.astype(BF16)) + dot(A.astype(BF16), v)
        b_last = b[R - 1:R]
        e_col = jnp.sum(jnp.where(eye, jnp.exp(b_last), 0.0), axis=1, keepdims=True)
        kd = pad_bf16(kk * jnp.exp(b_last - b))
        S_new = e_col * S + lax.dot_general(kd, v, TN, preferred_element_type=F32)
        S_ref[h] = S_new
        o_ref[:, sl] = (_rms(o, on_ref[...]) * jax.nn.silu(g_ref[:, sl])).astype(BF16)

        @pl.when(c == pl.num_programs(1) - 1)
        def _():
            sf_ref[h] = S_new


def hgrn(proj, lb, onorm, row0, nb, T, s0, layer, o_buf):
    R = min(T, HG_C)
    nc = T // R
    levels = int(math.log2(R))
    base = row0 // R
    BW = B_HEADS * LANE
    qcol = 3 * A_HEADS * LANE // BW
    mk = _hgrn_masks(levels, R)
    blk = lambda cb: pl.BlockSpec((R, BW), lambda b, c, cb=cb: (base + b * nc + c, cb))
    in_specs = [blk(qcol), blk(qcol + 1), blk(qcol + 2), blk(qcol + 3),
                pl.BlockSpec((1, BW), lambda b, c: (0, 0)),
                pl.BlockSpec((1, LANE), lambda b, c: (0, 0)),
                pl.BlockSpec(mk.shape, lambda b, c: (0, 0))]
    args = [proj, proj, proj, proj, lb.reshape(1, BW), onorm.reshape(1, LANE), mk]
    if s0 is not None:
        in_specs.append(pl.BlockSpec((None, None, B_HEADS, LANE, LANE), lambda b, c: (layer, b, 0, 0, 0)))
        args.append(s0)
    in_specs.append(pl.BlockSpec(memory_space=pl.ANY))
    args.append(o_buf)
    return pl.pallas_call(
        functools.partial(_hgrn_kernel, R=R, levels=levels, has_state=s0 is not None),
        grid=(nb, nc),
        in_specs=in_specs,
        out_specs=[pl.BlockSpec((R, BW), lambda b, c: (base + b * nc + c, 0)),
                   pl.BlockSpec((None, B_HEADS, LANE, LANE), lambda b, c: (b, 0, 0, 0))],
        out_shape=[jax.ShapeDtypeStruct(o_buf.shape, BF16),
                   jax.ShapeDtypeStruct((nb, B_HEADS, LANE, LANE), F32)],
        input_output_aliases={len(args) - 1: 0},
        scratch_shapes=[pltpu.VMEM((B_HEADS, LANE, LANE), F32)],
        compiler_params=_cparams(("parallel", "arbitrary")),
        name="hgrn",
    )(*args)


def _rg_gates(u, wa_ref, wx_ref, ba, bx, sp):
    ra, rx = [], []
    for n in range(C_BLOCKS):
        un = u[:, n * LANE:(n + 1) * LANE]
        ra.append(jnp.dot(un, wa_ref[n], precision=HI, preferred_element_type=F32))
        rx.append(jnp.dot(un, wx_ref[n], precision=HI, preferred_element_type=F32))
    r = jax.nn.sigmoid(jnp.concatenate(ra, axis=-1) + ba)
    ig = jax.nn.sigmoid(jnp.concatenate(rx, axis=-1) + bx)
    log_a = -RG_C * r * sp
    a = jnp.exp(log_a)
    z = 2.0 * log_a
    u2 = jnp.exp(z)
    em1 = jnp.where(u2 == 1.0, z, (u2 - 1.0) * z / jnp.log(u2))
    mult = jnp.sqrt(-em1)
    return a, mult, ig * u


def _scan_rows(a, b, t, n_steps):
    n = a.shape[0]
    for lv in range(n_steps):
        d = 1 << lv
        ok = t >= d
        a_s = jnp.where(ok, pltpu.roll(a, d, 0), 1.0)
        b_s = jnp.where(ok, pltpu.roll(b, d, 0), 0.0)
        b = a * b_s + b
        a = a * a_s
    return a, b


def _rglru_prompt_kernel(x_ref, gc_ref, cw_ref, cb_ref, wa_ref, wx_ref, ba_ref, bx_ref, lam_ref,
                         obuf_ref, o_ref, hf_ref, prev_ref, h_ref, *, Tc):
    del obuf_ref
    i = pl.program_id(1)

    @pl.when(i == 0)
    def _():
        prev_ref[...] = jnp.zeros(prev_ref.shape, F32)
        h_ref[...] = jnp.zeros(h_ref.shape, F32)

    x = x_ref[...]
    prev = prev_ref[...]
    t8 = lax.broadcasted_iota(jnp.int32, prev.shape, 0)
    cw = cw_ref[...]
    u = cb_ref[...] + x * cw[CONV_W - 1:CONV_W]
    for j in range(CONV_W - 1):
        d = CONV_W - 1 - j
        xs = pltpu.roll(x, d, 0)
        head = jnp.where(t8 >= d, xs[:SUBLANE], pltpu.roll(prev, d, 0))
        u = u + jnp.concatenate([head, xs[SUBLANE:]], axis=0) * cw[j:j + 1]
    prev_ref[...] = x[Tc - SUBLANE:]
    sp = jax.nn.softplus(-lam_ref[...])
    a, mult, iu = _rg_gates(u, wa_ref, wx_ref, ba_ref[...], bx_ref[...], sp)
    t = lax.broadcasted_iota(jnp.int32, a.shape, 0)
    mult = jnp.where((t == 0) & (i == 0), 1.0, mult)
    acc_a, acc_b = _scan_rows(a, mult * iu, t, int(math.log2(Tc)))
    h = acc_a * h_ref[0:1] + acc_b
    h_ref[...] = jnp.broadcast_to(h[Tc - 1:Tc], h_ref.shape)
    o_ref[...] = (h * jax.nn.gelu(gc_ref[...])).astype(BF16)

    @pl.when(i == pl.num_programs(1) - 1)
    def _():
        hf_ref[...] = h[Tc - 1:Tc]


def _rg_weight_specs(W, nidx):
    z = lambda shape: pl.BlockSpec(shape, lambda *_: (0,) * len(shape))
    return [z((CONV_W, W)), z((1, W)), z((C_BLOCKS, LANE, LANE)), z((C_BLOCKS, LANE, LANE)),
            z((1, W)), z((1, W)), z((1, W))]


def rglru_prompt(proj, wts, o_buf, B, S):
    W = C_BLOCKS * LANE
    Tc = _pick(S, (512, 256, 128))
    nt = S // Tc
    xcol = (3 * A_HEADS + 4 * B_HEADS) * LANE // W
    args = (proj, proj, *wts, o_buf)
    return pl.pallas_call(
        functools.partial(_rglru_prompt_kernel, Tc=Tc),
        grid=(B, nt),
        in_specs=[pl.BlockSpec((Tc, W), lambda b, i: (b * nt + i, xcol)),
                  pl.BlockSpec((Tc, W), lambda b, i: (b * nt + i, xcol + 1))] + _rg_weight_specs(W, 2)
                 + [pl.BlockSpec(memory_space=pl.ANY)],
        out_specs=[pl.BlockSpec((Tc, W), lambda b, i: (b * nt + i, 0)),
                   pl.BlockSpec((None, 1, W), lambda b, i: (b, 0, 0))],
        out_shape=[jax.ShapeDtypeStruct(o_buf.shape, BF16), jax.ShapeDtypeStruct((B, 1, W), F32)],
        input_output_aliases={len(args) - 1: 0},
        scratch_shapes=[pltpu.VMEM((SUBLANE, W), F32), pltpu.VMEM((SUBLANE, W), F32)],
        compiler_params=_cparams(("parallel", "arbitrary")),
        name="rglru_prompt",
    )(*args)


def _rglru_sample_kernel(x_ref, gc_ref, p_ref, h0_ref, cw_ref, cb_ref, wa_ref, wx_ref, ba_ref, bx_ref, lam_ref,
                         obuf_ref, o_ref, h_ref, *, Tq):
    del obuf_ref
    x = x_ref[...]
    n = x.shape[0]
    t = lax.broadcasted_iota(jnp.int32, x.shape, 0) % Tq
    p = p_ref[...]
    cw = cw_ref[...]
    u = cb_ref[...] + x * cw[CONV_W - 1:CONV_W]
    for j in range(CONV_W - 1):
        d = CONV_W - 1 - j
        shifted = jnp.where(t >= d, pltpu.roll(x, d, 0), pltpu.roll(p, n - (Tq - d), 0))
        u = u + shifted * cw[j:j + 1]
    sp = jax.nn.softplus(-lam_ref[...])
    a, mult, iu = _rg_gates(u, wa_ref, wx_ref, ba_ref[...], bx_ref[...], sp)
    acc_a, acc_b = _scan_rows(a, mult * iu, t, int(math.log2(Tq)))
    h = acc_a * h0_ref[...] + acc_b
    h_ref[...] = h
    o_ref[...] = (h * jax.nn.gelu(gc_ref[...])).astype(BF16)


def rglru_sample(proj, wts, pbuf, h0rep, o_buf, NP, NS, Tq):
    W = C_BLOCKS * LANE
    tr = _pick(NS, (256, 128, 64, 32, 16, 8))
    xcol = (3 * A_HEADS + 4 * B_HEADS) * LANE // W
    base = NP // tr
    loc = pl.BlockSpec((tr, W), lambda i: (i, 0))
    args = (proj, proj, pbuf, h0rep, *wts, o_buf)
    return pl.pallas_call(
        functools.partial(_rglru_sample_kernel, Tq=Tq),
        grid=(NS // tr,),
        in_specs=[pl.BlockSpec((tr, W), lambda i: (base + i, xcol)),
                  pl.BlockSpec((tr, W), lambda i: (base + i, xcol + 1)), loc, loc] + _rg_weight_specs(W, 1)
                 + [pl.BlockSpec(memory_space=pl.ANY)],
        out_specs=[pl.BlockSpec((tr, W), lambda i: (base + i, 0)), loc],
        out_shape=[jax.ShapeDtypeStruct(o_buf.shape, BF16), jax.ShapeDtypeStruct((NS, W), F32)],
        input_output_aliases={len(args) - 1: 0},
        compiler_params=_cparams(("parallel",)),
        name="rglru_sample",
    )(*args)


def _out_proj_kernel(x_ref, a_ref, b_ref, c_ref, wa_ref, wb_ref, wc_ref, o_ref):
    y = jnp.dot(a_ref[...], wa_ref[...], preferred_element_type=F32)
    y = y + jnp.dot(b_ref[...], wb_ref[...], preferred_element_type=F32)
    y = y + jnp.dot(c_ref[...], wc_ref[...], preferred_element_type=F32)
    o_ref[...] = x_ref[...] + y


def out_proj(x, oa, ob, oc, w):
    n, d = x.shape
    wa_, wb_, wc_ = oa.shape[1], ob.shape[1], oc.shape[1]
    assert wb_ == wc_ and wa_ % wb_ == 0
    tm = _pick(n, (1024, 512, 256, 128))
    tn = _pick(d, (512, 256, 128))
    return pl.pallas_call(
        _out_proj_kernel,
        grid=(n // tm, d // tn),
        in_specs=[pl.BlockSpec((tm, tn), lambda i, j: (i, j)),
                  pl.BlockSpec((tm, wa_), lambda i, j: (i, 0)),
                  pl.BlockSpec((tm, wb_), lambda i, j: (i, 0)),
                  pl.BlockSpec((tm, wc_), lambda i, j: (i, 0)),
                  pl.BlockSpec((wa_, tn), lambda i, j: (0, j)),
                  pl.BlockSpec((wb_, tn), lambda i, j: (wa_ // wb_, j)),
                  pl.BlockSpec((wc_, tn), lambda i, j: (wa_ // wb_ + 1, j))],
        out_specs=pl.BlockSpec((tm, tn), lambda i, j: (i, j)),
        out_shape=jax.ShapeDtypeStruct((n, d), F32),
        compiler_params=_cparams(("parallel", "arbitrary")),
        name="out_proj",
    )(x, oa, ob, oc, w, w, w)


def _ffn_up_kernel(te_ref, nu_ref, x_ref, g_ref, wg_ref, wu_ref, o_ref, h_ref, *, do_norm):
    i = pl.program_id(0)

    @pl.when(pl.program_id(1) == 0)
    def _():
        x = x_ref[...]
        if do_norm:
            x = _rms(x, g_ref[...])
        h_ref[...] = x.astype(BF16)

    @pl.when(i < nu_ref[0])
    def _():
        h = h_ref[...]
        a = jnp.dot(h, wg_ref[...], preferred_element_type=F32)
        b = jnp.dot(h, wu_ref[...], preferred_element_type=F32)
        o_ref[...] = (jax.nn.silu(a) * b).astype(BF16)

    @pl.when(i >= nu_ref[0])
    def _():
        o_ref[...] = jnp.zeros(o_ref.shape, BF16)


def ffn_up(x, g, wg, wu, te, nu, tm, do_norm):
    n, d = x.shape
    f = wg.shape[-1]
    tf = _pick(f, tuple(t for t in (1408, 512, 256, 128) if tm * t <= 512 * 1408))
    grid_spec = pltpu.PrefetchScalarGridSpec(
        num_scalar_prefetch=2,
        grid=(n // tm, f // tf),
        in_specs=[pl.BlockSpec((tm, d), lambda i, j, te, nu: (i, 0)),
                  pl.BlockSpec((1, d), lambda i, j, te, nu: (0, 0)),
                  pl.BlockSpec((None, d, tf), lambda i, j, te, nu: (te[i], 0, j)),
                  pl.BlockSpec((None, d, tf), lambda i, j, te, nu: (te[i], 0, j))],
        out_specs=pl.BlockSpec((tm, tf), lambda i, j, te, nu: (i, j)),
        scratch_shapes=[pltpu.VMEM((tm, d), BF16)])
    return pl.pallas_call(
        functools.partial(_ffn_up_kernel, do_norm=do_norm),
        grid_spec=grid_spec,
        out_shape=jax.ShapeDtypeStruct((n, f), BF16),
        compiler_params=_cparams(("parallel", "arbitrary")),
        name="ffn_up",
    )(te, nu, x, g.reshape(1, d), wg, wu)


def _ffn_down_kernel(te_ref, nu_ref, a_ref, wd_ref, *rest, residual):
    i = pl.program_id(0)
    if residual:
        r_ref, o_ref = rest
    else:
        (o_ref,) = rest

    @pl.when(i < nu_ref[0])
    def _():
        y = jnp.dot(a_ref[...], wd_ref[...], preferred_element_type=F32)
        if residual:
            y = r_ref[...] + y
        o_ref[...] = y

    @pl.when(i >= nu_ref[0])
    def _():
        o_ref[...] = jnp.zeros(o_ref.shape, F32)


def ffn_down(act, wd, te, nu, tm, res):
    n, f = act.shape
    d = wd.shape[-1]
    tn = _pick(d, tuple(t for t in (1024, 512, 256, 128) if tm * t <= 512 * 1024))
    in_specs = [pl.BlockSpec((tm, f), lambda i, j, te, nu: (i, 0)),
                pl.BlockSpec((None, f, tn), lambda i, j, te, nu: (te[i], 0, j))]
    args = [act, wd]
    if res is not None:
        in_specs.append(pl.BlockSpec((tm, tn), lambda i, j, te, nu: (i, j)))
        args.append(res)
    grid_spec = pltpu.PrefetchScalarGridSpec(
        num_scalar_prefetch=2,
        grid=(n // tm, d // tn),
        in_specs=in_specs,
        out_specs=pl.BlockSpec((tm, tn), lambda i, j, te, nu: (i, j)))
    return pl.pallas_call(
        functools.partial(_ffn_down_kernel, residual=res is not None),
        grid_spec=grid_spec,
        out_shape=jax.ShapeDtypeStruct((n, d), F32),
        compiler_params=_cparams(("parallel", "arbitrary")),
        name="ffn_down",
    )(te, nu, *args)


def _router_kernel(x_ref, g_ref, wr_ref, h_ref, rt_ref):
    h = _rms(x_ref[...], g_ref[...])
    h_ref[...] = h
    logits = jnp.dot(h, wr_ref[...], precision=HI, preferred_element_type=F32)
    lane = lax.broadcasted_iota(jnp.int32, logits.shape, 1)
    l1 = jnp.where(lane < N_EXPERTS, logits, -jnp.inf)
    m1 = jnp.max(l1, axis=-1, keepdims=True)
    i1 = jnp.min(jnp.where(l1 == m1, lane, LANE), axis=-1, keepdims=True)
    l2 = jnp.where(lane == i1, -jnp.inf, l1)
    m2 = jnp.max(l2, axis=-1, keepdims=True)
    i2 = jnp.min(jnp.where(l2 == m2, lane, LANE), axis=-1, keepdims=True)
    e = jnp.exp(m2 - m1)
    g1 = 1.0 / (1.0 + e)
    g2 = e / (1.0 + e)
    rt_ref[...] = jnp.where(lane == 0, i1.astype(F32),
                            jnp.where(lane == 1, i2.astype(F32),
                                      jnp.where(lane == 2, g1, jnp.where(lane == 3, g2, 0.0))))


def route_tokens(x, g, wr):
    n, d = x.shape
    tm = _pick(n, (512, 256, 128))
    wr_pad = jnp.zeros((d, LANE), F32).at[:, :N_EXPERTS].set(wr)
    return pl.pallas_call(
        _router_kernel,
        grid=(n // tm,),
        in_specs=[pl.BlockSpec((tm, d), lambda i: (i, 0)),
                  pl.BlockSpec((1, d), lambda i: (0, 0)),
                  pl.BlockSpec((d, LANE), lambda i: (0, 0))],
        out_specs=[pl.BlockSpec((tm, d), lambda i: (i, 0)), pl.BlockSpec((tm, LANE), lambda i: (i, 0))],
        out_shape=[jax.ShapeDtypeStruct((n, d), F32), jax.ShapeDtypeStruct((n, LANE), F32)],
        compiler_params=_cparams(("parallel",)),
        name="router",
    )(x, g.reshape(1, d), wr_pad)


DISPATCH_ROWS = 256
COMBINE_ROWS = 128
GATHER_UNROLL = 8


def _gather_start(idx_ref, idx0, src_ref, dst_ref, sem, rows):
    def issue(t, c):
        pltpu.make_async_copy(src_ref.at[pl.ds(idx_ref[idx0 + t], 1)], dst_ref.at[pl.ds(t, 1)], sem).start()
        return c
    lax.fori_loop(0, rows, issue, 0, unroll=GATHER_UNROLL)


def _gather_wait(src_ref, dst_ref, sem, rows):
    pltpu.make_async_copy(src_ref.at[pl.ds(0, rows)], dst_ref, sem).wait()


def _dispatch_kernel(src_ref, nrows_ref, h_ref, o_ref, buf_ref, sem):
    i = pl.program_id(0)
    R = DISPATCH_ROWS
    live = lambda s: s * R < nrows_ref[0]

    def start(s):
        _gather_start(src_ref, s * R, h_ref, buf_ref.at[s % 2], sem.at[s % 2], R)

    @pl.when(i == 0)
    def _():
        start(i)

    @pl.when((i + 1 < pl.num_programs(0)) & live(i + 1))
    def _():
        start(i + 1)

    @pl.when(live(i))
    def _():
        _gather_wait(h_ref, buf_ref.at[i % 2], sem.at[i % 2], R)
        o_ref[...] = buf_ref[i % 2].astype(BF16)

    @pl.when(jnp.logical_not(live(i)))
    def _():
        o_ref[...] = jnp.zeros(o_ref.shape, BF16)


def dispatch(h, src, nrows):
    n, d = h.shape
    p_rows = src.shape[0]
    assert p_rows % DISPATCH_ROWS == 0
    grid_spec = pltpu.PrefetchScalarGridSpec(
        num_scalar_prefetch=2,
        grid=(p_rows // DISPATCH_ROWS,),
        in_specs=[pl.BlockSpec(memory_space=pl.ANY)],
        out_specs=pl.BlockSpec((DISPATCH_ROWS, d), lambda i, src, nr: (i, 0)),
        scratch_shapes=[pltpu.VMEM((2, DISPATCH_ROWS, d), F32), pltpu.SemaphoreType.DMA((2,))])
    return pl.pallas_call(
        _dispatch_kernel,
        grid_spec=grid_spec,
        out_shape=jax.ShapeDtypeStruct((p_rows, d), BF16),
        compiler_params=_cparams(("arbitrary",)),
        name="moe_dispatch",
    )(src, nrows, h)


def _combine_kernel(pos_ref, x_ref, rt_ref, y_ref, o_ref, buf_ref, sem, *, n):
    i = pl.program_id(0)
    R = COMBINE_ROWS

    def start(s):
        for k in range(TOP_K):
            _gather_start(pos_ref, k * n + s * R, y_ref, buf_ref.at[s % 2, k], sem.at[s % 2], R)

    @pl.when(i == 0)
    def _():
        start(i)

    @pl.when(i + 1 < pl.num_programs(0))
    def _():
        start(i + 1)

    for k in range(TOP_K):
        _gather_wait(y_ref, buf_ref.at[i % 2, k], sem.at[i % 2], R)
    rt = rt_ref[...]
    o_ref[...] = x_ref[...] + rt[:, 2:3] * buf_ref[i % 2, 0] + rt[:, 3:4] * buf_ref[i % 2, 1]


def combine(x, rt, y, pos):
    n, d = x.shape
    assert n % COMBINE_ROWS == 0
    grid_spec = pltpu.PrefetchScalarGridSpec(
        num_scalar_prefetch=1,
        grid=(n // COMBINE_ROWS,),
        in_specs=[pl.BlockSpec((COMBINE_ROWS, d), lambda i, pos: (i, 0)),
                  pl.BlockSpec((COMBINE_ROWS, LANE), lambda i, pos: (i, 0)),
                  pl.BlockSpec(memory_space=pl.ANY)],
        out_specs=pl.BlockSpec((COMBINE_ROWS, d), lambda i, pos: (i, 0)),
        scratch_shapes=[pltpu.VMEM((2, TOP_K, COMBINE_ROWS, d), F32), pltpu.SemaphoreType.DMA((2,))])
    return pl.pallas_call(
        functools.partial(_combine_kernel, n=n),
        grid_spec=grid_spec,
        out_shape=jax.ShapeDtypeStruct((n, d), F32),
        compiler_params=_cparams(("arbitrary",)),
        name="moe_combine",
    )(pos, x, rt, y)


def _route_plan(rt, tm, n_tiles):
    n = rt.shape[0]
    e_flat = jnp.concatenate([rt[:, 0], rt[:, 1]]).astype(jnp.int32)
    onehot = (e_flat[:, None] == jnp.arange(N_EXPERTS)[None, :]).astype(jnp.int32)
    csum = jnp.cumsum(onehot, axis=0)
    rank = jnp.sum(csum * onehot, axis=1) - 1
    counts = csum[-1]
    padded = ((counts + tm - 1) // tm) * tm
    gend = jnp.cumsum(padded)
    gstart = gend - padded
    pos = jnp.sum(onehot * gstart[None, :], axis=1) + rank
    nu = (gend[-1] // tm).astype(jnp.int32).reshape(1)
    tile_start = jnp.arange(n_tiles, dtype=jnp.int32) * tm
    te = jnp.sum((tile_start[:, None] >= gend[None, :]).astype(jnp.int32), axis=1)
    last = jnp.max(jnp.where(counts > 0, jnp.arange(N_EXPERTS), 0))
    te = jnp.minimum(te, last).astype(jnp.int32)
    pos = pos.astype(jnp.int32)
    tok = jnp.arange(TOP_K * n, dtype=jnp.int32) % n
    src = jnp.zeros((n_tiles * tm,), jnp.int32).at[pos].set(tok)
    return pos, src, te, nu


def ffn_moe(x, g, wr, wg, wu, wd):
    n, d = x.shape
    tm = _pick(n, (512, 256, 128))
    n_tiles = (TOP_K * n) // tm + N_EXPERTS
    h, rt = route_tokens(x, g, wr)
    pos, src, te, nu = _route_plan(rt, tm, n_tiles)
    xs = dispatch(h, src, nu * tm)
    act = ffn_up(xs, g, wg, wu, te, nu, tm, do_norm=False)
    y = ffn_down(act, wd, te, nu, tm, None)
    return combine(x, rt, y, pos)


def ffn_dense(x, g, wg, wu, wd):
    n, d = x.shape
    tm = _pick(n, (1024, 512, 256, 128))
    te = jnp.zeros((n // tm,), jnp.int32)
    nu = jnp.full((1,), n // tm, jnp.int32)
    act = ffn_up(x, g, wg[None], wu[None], te, nu, tm, do_norm=True)
    return ffn_down(act, wd[None], te, nu, tm, x)


def kernel(x_prompt, x_sample, cache_k, cache_v, page_table, state_hgrn, state_rglru, state_conv,
           w_in, w_out, g_mix, g_ffn, q_norm, k_norm, lam_q1, lam_k1, lam_q2, lam_k2, subln,
           rel_table, hgrn_lb, hgrn_onorm, conv_w, conv_b, rg_wa, rg_ba, rg_wx, rg_bx, rg_lambda,
           ff_gate, ff_up, ff_down, router, ex_gate, ex_up, ex_down):
    B, S, D = x_prompt.shape
    DB, Tq, _ = x_sample.shape
    depth = w_in.shape[0]
    NP, NS = B * S, DB * Tq
    AW = A_HEADS * LANE
    CW = C_BLOCKS * LANE
    xc0 = 3 * AW + 4 * B_HEADS * LANE
    x = jnp.concatenate([x_prompt.reshape(NP, D), x_sample.reshape(NS, D)], axis=0)

    sm = jax.nn.softmax(hgrn_lb.astype(F32), axis=0)
    lb_all = jnp.cumsum(sm, axis=0) - sm[0:1]

    outs = {k: [] for k in ("kp", "vp", "sp", "hp", "bp", "ks", "vs", "ss", "hs", "bs")}
    for l in range(depth):
        lam_init = 0.8 - 0.6 * math.exp(-0.3 * l)
        lamp = jnp.stack([lam_q1[l], lam_k1[l], lam_q2[l], lam_k2[l]]).astype(F32)
        proj = norm_matmul(x, g_mix[l], w_in[l].astype(BF16))
        ta = _pick(math.gcd(S, NP + NS), (512, 256, 128))
        qf, kf, kb, qt, vt = qk_prep(proj, q_norm[l], k_norm[l], ta)
        vf = proj[:, 2 * AW:3 * AW]
        oa = attn_prompt(qt, kb, vt, rel_table, lamp, subln[l], jnp.zeros((NP + NS, AW), BF16), B, S, ta, lam_init)
        kn2 = kf[NP:].reshape(DB, Tq * A_HEADS, LANE)
        vn2 = vf[NP:].reshape(DB, Tq * A_HEADS, LANE)
        oa = attn_sample(qf, kn2, vn2, cache_k, cache_v, l, page_table, rel_table, lamp, subln[l], oa,
                         NP, DB, Tq, lam_init)
        ob, s_p = hgrn(proj, lb_all[l], hgrn_onorm[l], 0, B, S, None, l, jnp.zeros((NP + NS, B_HEADS * LANE), BF16))
        ob, s_s = hgrn(proj, lb_all[l], hgrn_onorm[l], NP, DB, Tq, state_hgrn, l, ob)
        wts = (conv_w[l], conv_b[l].reshape(1, CW), rg_wa[l], rg_wx[l], rg_ba[l].reshape(1, CW),
               rg_bx[l].reshape(1, CW), rg_lambda[l].reshape(1, CW))
        oc, h_p = rglru_prompt(proj, wts, jnp.zeros((NP + NS, CW), BF16), B, S)
        pbuf = jnp.concatenate([jnp.zeros((DB, Tq - (CONV_W - 1), CW), F32), state_conv[l]], axis=1).reshape(NS, CW)
        h0rep = jnp.repeat(state_rglru[l], Tq, axis=0)
        oc, h_s = rglru_sample(proj, wts, pbuf, h0rep, oc, NP, NS, Tq)
        x = out_proj(x, oa, ob, oc, w_out[l].astype(BF16))
        m = l // 2
        if l % 2 == 0:
            x = ffn_dense(x, g_ffn[l], ff_gate[m].astype(BF16), ff_up[m].astype(BF16), ff_down[m].astype(BF16))
        else:
            x = ffn_moe(x, g_ffn[l], router[m], ex_gate[m].astype(BF16), ex_up[m].astype(BF16),
                        ex_down[m].astype(BF16))
        xcs = proj[:, xc0:xc0 + CW]
        outs["kp"].append(kf[:NP].reshape(B, S, A_HEADS, LANE))
        outs["vp"].append(vf[:NP].reshape(B, S, A_HEADS, LANE))
        outs["sp"].append(s_p)
        outs["hp"].append(h_p.reshape(B, CW))
        outs["bp"].append(xcs[:NP].reshape(B, S, CW)[:, S - (CONV_W - 1):])
        outs["ks"].append(kf[NP:].reshape(DB, Tq, A_HEADS, LANE))
        outs["vs"].append(vf[NP:].reshape(DB, Tq, A_HEADS, LANE))
        outs["ss"].append(s_s)
        outs["hs"].append(h_s.reshape(DB, Tq, CW)[:, Tq - 1])
        outs["bs"].append(xcs[NP:].reshape(DB, Tq, CW)[:, Tq - (CONV_W - 1):])
    st = {k: jnp.stack(v) for k, v in outs.items()}
    return (x[:NP].reshape(B, S, D), x[NP:].reshape(DB, Tq, D),
            st["kp"], st["vp"], st["sp"], st["hp"], st["bp"],
            st["ks"], st["vs"], st["ss"], st["hs"], st["bs"])
```

```python
import functools
import math

import jax
import jax.numpy as jnp
from jax import lax
from jax.experimental import pallas as pl
from jax.experimental.pallas import tpu as pltpu

F32 = jnp.float32
BF16 = jnp.bfloat16
HI = lax.Precision.HIGHEST
EPS = 1e-6
NEG = -1e30
LANE = 128
SUBLANE = 8
NT = (((1,), (1,)), ((), ()))
TN = (((0,), (0,)), ((), ()))

A_HEADS = 8
A_SUB = 64
A_VDIM = 128
A_SCALE = A_SUB ** -0.5
LOG2E = math.log2(math.e)
Q_SCALE = A_SCALE * LOG2E
B_HEADS = 4
C_BLOCKS = 4
CONV_W = 4
RG_C = 8.0
PAGE = 128
N_BUCKETS = 32
MAX_EXACT = 16
MAX_DISTANCE = 128
N_EXPERTS = 8
TOP_K = 2


def _cparams(sem, vmem_mb=48):
    return pltpu.CompilerParams(dimension_semantics=sem, vmem_limit_bytes=vmem_mb << 20)


def _pick(n, prefs):
    for p in prefs:
        if n % p == 0:
            return p
    raise ValueError(f"no tile in {prefs} divides {n}")


def _rms(x, g):
    return x * lax.rsqrt(jnp.mean(x * x, axis=-1, keepdims=True) + EPS) * g


def _norm_matmul_kernel(x_ref, g_ref, w_ref, o_ref, h_ref):
    @pl.when(pl.program_id(1) == 0)
    def _():
        h_ref[...] = _rms(x_ref[...], g_ref[...]).astype(BF16)

    o_ref[...] = jnp.dot(h_ref[...], w_ref[...], preferred_element_type=F32)


def norm_matmul(x, g, w):
    n, d = x.shape
    c = w.shape[1]
    tm = _pick(n, (1024, 512, 256, 128))
    tn = _pick(c, (1024, 512, 256, 128))
    return pl.pallas_call(
        _norm_matmul_kernel,
        grid=(n // tm, c // tn),
        in_specs=[pl.BlockSpec((tm, d), lambda i, j: (i, 0)),
                  pl.BlockSpec((1, d), lambda i, j: (0, 0)),
                  pl.BlockSpec((d, tn), lambda i, j: (0, j))],
        out_specs=pl.BlockSpec((tm, tn), lambda i, j: (i, j)),
        out_shape=jax.ShapeDtypeStruct((n, c), F32),
        scratch_shapes=[pltpu.VMEM((tm, d), BF16)],
        compiler_params=_cparams(("parallel", "arbitrary")),
        name="norm_matmul",
    )(x, g.reshape(1, d), w)


def _qk_prep_kernel(q_ref, k_ref, v_ref, qg_ref, kg_ref, qf_ref, kf_ref, kb_ref, qt_ref, vt_ref):
    r = lax.broadcasted_iota(jnp.int32, (LANE, LANE), 0)
    c = lax.broadcasted_iota(jnp.int32, (LANE, LANE), 1)
    seg = jnp.where((r // A_SUB) == (c // A_SUB), 1.0 / A_SUB, 0.0).astype(F32)

    def normed(src, g_ref, sl):
        x = src[:, sl]
        ms = jnp.dot(x * x, seg, precision=HI, preferred_element_type=F32)
        return x * lax.rsqrt(ms + EPS) * g_ref[...]

    for h in range(A_HEADS):
        sl = slice(h * LANE, (h + 1) * LANE)
        q = normed(q_ref, qg_ref, sl) * Q_SCALE
        qf_ref[:, sl] = q
        qt_ref[h] = q.T.astype(BF16)
        k = normed(k_ref, kg_ref, sl)
        kf_ref[:, sl] = k
        kb_ref[:, sl] = k.astype(BF16)
        vt_ref[h] = v_ref[:, sl].T.astype(BF16)


def qk_prep(proj, q_norm, k_norm, tm):
    n = proj.shape[0]
    w = A_HEADS * LANE
    qg = jnp.concatenate([q_norm, q_norm]).reshape(1, LANE)
    kg = jnp.concatenate([k_norm, k_norm]).reshape(1, LANE)
    col = lambda cb: pl.BlockSpec((tm, w), lambda i, cb=cb: (i, cb))
    vec = pl.BlockSpec((1, LANE), lambda i: (0, 0))
    out = pl.BlockSpec((tm, w), lambda i: (i, 0))
    out_t = pl.BlockSpec((A_HEADS, None, LANE, tm), lambda i: (0, i, 0, 0))
    t_shape = jax.ShapeDtypeStruct((A_HEADS, n // tm, LANE, tm), BF16)
    return pl.pallas_call(
        _qk_prep_kernel,
        grid=(n // tm,),
        in_specs=[col(0), col(1), col(2), vec, vec],
        out_specs=[out, out, out, out_t, out_t],
        out_shape=[jax.ShapeDtypeStruct((n, w), F32)] * 2 + [jax.ShapeDtypeStruct((n, w), BF16), t_shape, t_shape],
        compiler_params=_cparams(("parallel",)),
        name="qk_prep",
    )(proj, proj, proj, qg, kg)


def _rel_bias(table, n):
    nf = jnp.maximum(n, 1).astype(F32)
    far = MAX_EXACT + (jnp.log(nf / MAX_EXACT) / math.log(MAX_DISTANCE / MAX_EXACT)
                       * (N_BUCKETS - MAX_EXACT)).astype(jnp.int32)
    bucket = jnp.where(n < MAX_EXACT, n, jnp.minimum(far, N_BUCKETS - 1))
    onehot = (bucket.reshape(1, -1) == jnp.arange(N_BUCKETS)[:, None]).astype(F32)
    t = table.astype(F32)
    vals = jnp.dot(t.T, onehot, precision=HI)
    return ((vals - t[N_BUCKETS - 1][:, None]) * LOG2E).reshape((t.shape[1],) + n.shape)


def _lam_from(lamp, lam_init):
    s1 = jnp.sum(lamp[0:1] * lamp[1:2], axis=-1, keepdims=True)
    s2 = jnp.sum(lamp[2:3] * lamp[3:4], axis=-1, keepdims=True)
    return jnp.exp(s1) - jnp.exp(s2) + lam_init


def _attn_prompt_kernel(q_ref, k_ref, v_ref, bias_ref, lamp_ref, subln_ref, obuf_ref, o_ref,
                        m_ref, l_ref, acc_ref, *, T, lam_init):
    del obuf_ref
    qi = pl.program_id(2)
    qt = q_ref[...]
    sub = lax.broadcasted_iota(jnp.int32, qt.shape, 0)
    zero = jnp.zeros_like(qt)
    qs = (jnp.where(sub < A_SUB, qt, zero), jnp.where(sub >= A_SUB, qt, zero))
    m_ref[...] = jnp.full(m_ref.shape, NEG, F32)
    l_ref[...] = jnp.zeros(l_ref.shape, F32)
    acc_ref[...] = jnp.zeros(acc_ref.shape, F32)

    def tile(kt, bias, width=1):
        k = k_ref[pl.ds(pl.multiple_of(kt * T, T), width * T), :]
        vts = [v_ref[kt + w] for w in range(width)]
        for c in range(2):
            s = jnp.dot(k, qs[c], preferred_element_type=F32)
            if bias is not None:
                s = s + bias
            m_old = m_ref[c]
            m_new = jnp.maximum(m_old, jnp.max(s, axis=0, keepdims=True))
            alpha = jnp.exp2(m_old - m_new)
            p = jnp.exp2(s - m_new)
            l_ref[c] = alpha * l_ref[c] + jnp.sum(p, axis=0, keepdims=True)
            pb = p.astype(BF16)
            acc = alpha * acc_ref[c]
            for w in range(width):
                acc = acc + jnp.dot(vts[w], pb[w * T:(w + 1) * T], preferred_element_type=F32)
            acc_ref[c] = acc
            m_ref[c] = m_new

    n_far = jnp.maximum(qi - 1, 0)

    def far_pair(pr, carry):
        tile(2 * pr, None, width=2)
        return carry

    lax.fori_loop(0, n_far // 2, far_pair, 0)

    @pl.when(n_far % 2 == 1)
    def _():
        tile(n_far - 1, None)

    nb = T // LANE
    same, after = bias_ref[0], bias_ref[1]
    zero_b = jnp.zeros((LANE, LANE), F32)
    neg_b = jnp.full((LANE, LANE), NEG, F32)
    grid_of = lambda pick: jnp.concatenate(
        [jnp.concatenate([pick(jk, iq) for iq in range(nb)], axis=1) for jk in range(nb)], axis=0)

    @pl.when(qi >= 1)
    def _():
        tile(qi - 1, grid_of(lambda jk, iq: after if (jk == nb - 1 and iq == 0) else zero_b))

    tile(qi, grid_of(lambda jk, iq: neg_b if iq < jk else same if iq == jk else after if iq == jk + 1 else zero_b))

    lam = _lam_from(lamp_ref[...], lam_init)
    ot = acc_ref[0] / l_ref[0] - lam * (acc_ref[1] / l_ref[1])
    ms = jnp.mean(ot * ot, axis=0, keepdims=True)
    y = ot * lax.rsqrt(ms + EPS) * subln_ref[...] * (1.0 - lam_init)
    o_ref[...] = y.T.astype(BF16)


def attn_prompt(qt, kb, vt, rel_table, lamp, subln, o_buf, B, S, T, lam_init):
    assert T >= MAX_DISTANCE and S % T == 0
    nq = S // T
    assert LANE == MAX_DISTANCE
    d = jnp.arange(LANE)[None, :] - jnp.arange(LANE)[:, None]
    bias = jnp.stack([jnp.where(d >= 0, _rel_bias(rel_table, jnp.maximum(d, 0)), NEG),
                      _rel_bias(rel_table, d + LANE)], axis=1)
    return pl.pallas_call(
        functools.partial(_attn_prompt_kernel, T=T, lam_init=lam_init),
        grid=(B, A_HEADS, nq),
        in_specs=[pl.BlockSpec((None, None, LANE, T), lambda b, h, i: (h, b * nq + i, 0, 0)),
                  pl.BlockSpec((S, LANE), lambda b, h, i: (b, h)),
                  pl.BlockSpec((None, nq, LANE, T), lambda b, h, i: (h, b, 0, 0)),
                  pl.BlockSpec((None, 2, LANE, LANE), lambda b, h, i: (h, 0, 0, 0)),
                  pl.BlockSpec((4, A_SUB), lambda b, h, i: (0, 0)),
                  pl.BlockSpec((LANE, 1), lambda b, h, i: (0, 0)),
                  pl.BlockSpec(memory_space=pl.ANY)],
        out_specs=pl.BlockSpec((T, LANE), lambda b, h, i: (b * nq + i, h)),
        out_shape=jax.ShapeDtypeStruct(o_buf.shape, BF16),
        input_output_aliases={6: 0},
        scratch_shapes=[pltpu.VMEM((2, 1, T), F32), pltpu.VMEM((2, 1, T), F32),
                        pltpu.VMEM((2, LANE, T), F32)],
        compiler_params=_cparams(("parallel", "parallel", "arbitrary")),
        name="attn_prompt",
    )(qt, kb, vt, bias, lamp, subln.reshape(LANE, 1), o_buf)


def _attn_sample_kernel(pt_ref, q_ref, kn_ref, vn_ref, mask_ref, bnew_ref, lamp_ref, subln_ref, *rest,
                        PPS, n_steps, lam_init):
    k_refs = rest[:PPS]
    v_refs = rest[PPS:2 * PPS]
    o_ref = rest[2 * PPS + 1]
    qm_ref, m_ref, l_ref, acc_ref, s_ref = rest[2 * PPS + 2:]
    L = PAGE * A_HEADS
    step = pl.program_id(1)
    Tq = q_ref.shape[0]

    @pl.when(step == 0)
    def _():
        q = q_ref[...]
        lane = lax.broadcasted_iota(jnp.int32, (Tq, LANE), 1)
        rows = []
        for h in range(A_HEADS):
            qh = q[:, h * LANE:(h + 1) * LANE]
            rows.append(jnp.where(lane < A_SUB, qh, 0.0))
            rows.append(jnp.where(lane >= A_SUB, qh, 0.0))
        qm_ref[...] = jnp.concatenate(rows, axis=0).astype(BF16)
        m_ref[...] = jnp.full(m_ref.shape, NEG, F32)
        l_ref[...] = jnp.zeros(l_ref.shape, F32)
        acc_ref[...] = jnp.zeros(acc_ref.shape, F32)

    qm = qm_ref[...]
    mel = None
    for j in range(PPS):
        k2 = k_refs[j][...].reshape(L, LANE).astype(BF16)
        s = lax.dot_general(qm, k2, NT, preferred_element_type=F32)
        if j == PPS - 1:
            s = s + mask_ref[jnp.where(step == n_steps - 1, 1, 0)]
        else:
            s = s + mask_ref[0]
        s_ref[:, j * L:(j + 1) * L] = s
        mel = s if mel is None else jnp.maximum(mel, s)
    m_old = m_ref[...]
    m_new = jnp.maximum(m_old, jnp.max(mel, axis=-1, keepdims=True))
    alpha = jnp.exp2(m_old - m_new)
    lsum = None
    acc = alpha * acc_ref[...]
    for j in range(PPS):
        p = jnp.exp2(s_ref[:, j * L:(j + 1) * L] - m_new)
        lsum = p if lsum is None else lsum + p
        v2 = v_refs[j][...].reshape(L, LANE).astype(BF16)
        acc = acc + jnp.dot(p.astype(BF16), v2, preferred_element_type=F32)
    l_ref[...] = alpha * l_ref[...] + jnp.sum(lsum, axis=-1, keepdims=True)
    acc_ref[...] = acc
    m_ref[...] = m_new

    @pl.when(step == n_steps - 1)
    def _():
        kn2 = kn_ref[...].astype(BF16)
        vn2 = vn_ref[...].astype(BF16)
        s = lax.dot_general(qm, kn2, NT, preferred_element_type=F32) + bnew_ref[...]
        m_fin = jnp.maximum(m_new, jnp.max(s, axis=-1, keepdims=True))
        a_fin = jnp.exp2(m_new - m_fin)
        p = jnp.exp2(s - m_fin)
        l_ref[...] = a_fin * l_ref[...] + jnp.sum(p, axis=-1, keepdims=True)
        acc_ref[...] = a_fin * acc_ref[...] + jnp.dot(p.astype(BF16), vn2, preferred_element_type=F32)
        lam = _lam_from(lamp_ref[...], lam_init)
        o = acc_ref[...] / l_ref[...]
        for h in range(A_HEADS):
            r0 = h * 2 * Tq
            oh = o[r0:r0 + Tq] - lam * o[r0 + Tq:r0 + 2 * Tq]
            o_ref[:, h * LANE:(h + 1) * LANE] = (_rms(oh, subln_ref[...]) * (1.0 - lam_init)).astype(BF16)


def attn_sample(qf, kn2, vn2, cache_k, cache_v, layer, page_table, rel_table, lamp, subln, o_buf,
                NP, DB, Tq, lam_init):
    n_pages = page_table.shape[1]
    PPS = _pick(n_pages, (16, 8, 4, 2, 1))
    n_steps = n_pages // PPS
    assert Tq <= PAGE and PAGE >= MAX_DISTANCE
    R = A_HEADS * 2 * Tq
    L = PAGE * A_HEADS
    t_ = jnp.arange(Tq)[:, None]
    b_last = _rel_bias(rel_table, PAGE + t_ - jnp.arange(PAGE)[None, :])
    d_new = t_ - jnp.arange(Tq)[None, :]
    b_new = jnp.where(d_new >= 0, _rel_bias(rel_table, jnp.maximum(d_new, 0)), NEG)
    same = (jnp.arange(A_HEADS)[:, None] == jnp.arange(A_HEADS)[None, :])[:, None, None, None, :]
    expand = lambda b, nk: jnp.where(same, jnp.broadcast_to(b[:, None, :, :, None], (A_HEADS, 2, Tq, nk, A_HEADS)),
                                     NEG).reshape(R, nk * A_HEADS)
    masks = jnp.stack([expand(jnp.zeros((A_HEADS, Tq, PAGE), F32), PAGE), expand(b_last, PAGE)])
    bnew = expand(b_new, Tq)

    pt_flat = page_table.reshape(-1).astype(jnp.int32)
    page_spec = lambda j: pl.BlockSpec(
        (None, None, PAGE, A_HEADS, LANE),
        lambda b, s, pt, j=j: (layer, pt[b * n_pages + s * PPS + j], 0, 0, 0))
    const2 = lambda shape: pl.BlockSpec(shape, lambda b, s, pt: (0,) * len(shape))
    in_specs = ([pl.BlockSpec((Tq, A_HEADS * LANE), lambda b, s, pt: (NP // Tq + b, 0)),
                 pl.BlockSpec((None, Tq * A_HEADS, LANE), lambda b, s, pt: (b, 0, 0)),
                 pl.BlockSpec((None, Tq * A_HEADS, LANE), lambda b, s, pt: (b, 0, 0)),
                 const2((2, R, L)), const2((R, Tq * A_HEADS)), const2((4, A_SUB)), const2((1, LANE))]
                + [page_spec(j) for j in range(PPS)] * 2 + [pl.BlockSpec(memory_space=pl.ANY)])
    grid_spec = pltpu.PrefetchScalarGridSpec(
        num_scalar_prefetch=1,
        grid=(DB, n_steps),
        in_specs=in_specs,
        out_specs=pl.BlockSpec((Tq, A_HEADS * LANE), lambda b, s, pt: (NP // Tq + b, 0)),
        scratch_shapes=[pltpu.VMEM((R, LANE), BF16), pltpu.VMEM((R, 1), F32), pltpu.VMEM((R, 1), F32),
                        pltpu.VMEM((R, LANE), F32), pltpu.VMEM((R, PPS * L), F32)])
    return pl.pallas_call(
        functools.partial(_attn_sample_kernel, PPS=PPS, n_steps=n_steps, lam_init=lam_init),
        grid_spec=grid_spec,
        out_shape=jax.ShapeDtypeStruct(o_buf.shape, BF16),
        input_output_aliases={8 + 2 * PPS: 0},
        compiler_params=_cparams(("parallel", "arbitrary"), vmem_mb=56),
        name="attn_sample",
    )(pt_flat, qf, kn2, vn2, masks, bnew, lamp, subln.reshape(1, LANE),
      *([cache_k] * PPS), *([cache_v] * PPS), o_buf)


HG_C = 128


def _hgrn_masks(levels, R):
    row = jnp.arange(R)[:, None]
    col = jnp.arange(HG_C)[None, :]
    blocks = [row >= col]
    for lv in range(levels):
        s = 1 << lv
        mid = (row // (2 * s)) * (2 * s) + (s - 1)
        second = (row % (2 * s)) >= s
        blocks.append(jnp.where(second, (col > mid) & (col <= row), (col > row) & (col <= mid)))
    return jnp.concatenate(blocks, axis=0).astype(BF16)


def _hgrn_kernel(*refs, R, levels, has_state):
    if has_state:
        q_ref, f_ref, i_ref, g_ref, lb_ref, on_ref, mk_ref, s0_ref, _, o_ref, sf_ref, S_ref = refs
    else:
        q_ref, f_ref, i_ref, g_ref, lb_ref, on_ref, mk_ref, _, o_ref, sf_ref, S_ref = refs
    c = pl.program_id(1)

    @pl.when(c == 0)
    def _():
        S_ref[...] = s0_ref[...] if has_state else jnp.zeros(S_ref.shape, F32)

    row = lax.broadcasted_iota(jnp.int32, (R, HG_C), 0)
    col = lax.broadcasted_iota(jnp.int32, (R, HG_C), 1)
    eye = (lax.broadcasted_iota(jnp.int32, (LANE, LANE), 0) == lax.broadcasted_iota(jnp.int32, (LANE, LANE), 1))
    mk = mk_ref[...]
    dot = functools.partial(jnp.dot, preferred_element_type=F32)

    def pad_bf16(x):
        if R < HG_C:
            x = jnp.concatenate([x, jnp.zeros((HG_C - R, x.shape[1]), F32)], axis=0)
        return x.astype(BF16)

    for h in range(B_HEADS):
        sl = slice(h * LANE, (h + 1) * LANE)
        lb = lb_ref[:, sl]
        f = lb + (1.0 - lb) * jax.nn.sigmoid(f_ref[:, sl])
        g = jnp.log(f)
        kk = 1.0 - f
        qq = jax.nn.silu(q_ref[:, sl])
        v = pad_bf16(i_ref[:, sl])
        g0 = g.astype(BF16).astype(F32)
        r1 = g - g0
        g1 = r1.astype(BF16).astype(F32)
        e3 = dot(mk, pad_bf16(jnp.concatenate([g0, g1, r1 - g1], axis=-1)))
        e = e3[:, :LANE] + e3[:, LANE:2 * LANE] + e3[:, 2 * LANE:]
        b = e[:R]
        A = jnp.where(row == col, jnp.sum(qq * kk, axis=-1, keepdims=True), 0.0)
        for lv in range(levels):
            s = 1 << lv
            x = jnp.exp(e[(lv + 1) * R:(lv + 2) * R])
            second = (row % (2 * s)) >= s
            ql = jnp.where(second, qq * x, 0.0).astype(BF16)
            kl = pad_bf16(jnp.where(second, 0.0, kk * x))
            al = lax.dot_general(ql, kl, NT, preferred_element_type=F32)
            A = A + jnp.where((row // (2 * s)) == (col // (2 * s)), al, 0.0)
        S = S_ref[h]
        o = dot((qq * jnp.exp(b)).astype(BF16), S.astype(BF16)) + dot(A.astype(BF16), v)
        b_last = b[R - 1:R]
        e_col = jnp.sum(jnp.where(eye, jnp.exp(b_last), 0.0), axis=1, keepdims=True)
        kd = pad_bf16(kk * jnp.exp(b_last - b))
        S_new = e_col * S + lax.dot_general(kd, v, TN, preferred_element_type=F32)
        S_ref[h] = S_new
        o_ref[:, sl] = (_rms(o, on_ref[...]) * jax.nn.silu(g_ref[:, sl])).astype(BF16)

        @pl.when(c == pl.num_programs(1) - 1)
        def _():
            sf_ref[h] = S_new


def hgrn(proj, lb, onorm, row0, nb, T, s0, layer, o_buf):
    R = min(T, HG_C)
    nc = T // R
    levels = int(math.log2(R))
    base = row0 // R
    BW = B_HEADS * LANE
    qcol = 3 * A_HEADS * LANE // BW
    mk = _hgrn_masks(levels, R)
    blk = lambda cb: pl.BlockSpec((R, BW), lambda b, c, cb=cb: (base + b * nc + c, cb))
    in_specs = [blk(qcol), blk(qcol + 1), blk(qcol + 2), blk(qcol + 3),
                pl.BlockSpec((1, BW), lambda b, c: (0, 0)),
                pl.BlockSpec((1, LANE), lambda b, c: (0, 0)),
                pl.BlockSpec(mk.shape, lambda b, c: (0, 0))]
    args = [proj, proj, proj, proj, lb.reshape(1, BW), onorm.reshape(1, LANE), mk]
    if s0 is not None:
        in_specs.append(pl.BlockSpec((None, None, B_HEADS, LANE, LANE), lambda b, c: (layer, b, 0, 0, 0)))
        args.append(s0)
    in_specs.append(pl.BlockSpec(memory_space=pl.ANY))
    args.append(o_buf)
    return pl.pallas_call(
        functools.partial(_hgrn_kernel, R=R, levels=levels, has_state=s0 is not None),
        grid=(nb, nc),
        in_specs=in_specs,
        out_specs=[pl.BlockSpec((R, BW), lambda b, c: (base + b * nc + c, 0)),
                   pl.BlockSpec((None, B_HEADS, LANE, LANE), lambda b, c: (b, 0, 0, 0))],
        out_shape=[jax.ShapeDtypeStruct(o_buf.shape, BF16),
                   jax.ShapeDtypeStruct((nb, B_HEADS, LANE, LANE), F32)],
        input_output_aliases={len(args) - 1: 0},
        scratch_shapes=[pltpu.VMEM((B_HEADS, LANE, LANE), F32)],
        compiler_params=_cparams(("parallel", "arbitrary")),
        name="hgrn",
    )(*args)


def _rg_gates(u, wa_ref, wx_ref, ba, bx, sp):
    ra, rx = [], []
    for n in range(C_BLOCKS):
        un = u[:, n * LANE:(n + 1) * LANE]
        ra.append(jnp.dot(un, wa_ref[n], precision=HI, preferred_element_type=F32))
        rx.append(jnp.dot(un, wx_ref[n], precision=HI, preferred_element_type=F32))
    r = jax.nn.sigmoid(jnp.concatenate(ra, axis=-1) + ba)
    ig = jax.nn.sigmoid(jnp.concatenate(rx, axis=-1) + bx)
    log_a = -RG_C * r * sp
    a = jnp.exp(log_a)
    z = 2.0 * log_a
    u2 = jnp.exp(z)
    em1 = jnp.where(u2 == 1.0, z, (u2 - 1.0) * z / jnp.log(u2))
    mult = jnp.sqrt(-em1)
    return a, mult, ig * u


def _scan_rows(a, b, t, n_steps):
    n = a.shape[0]
    for lv in range(n_steps):
        d = 1 << lv
        ok = t >= d
        a_s = jnp.where(ok, pltpu.roll(a, d, 0), 1.0)
        b_s = jnp.where(ok, pltpu.roll(b, d, 0), 0.0)
        b = a * b_s + b
        a = a * a_s
    return a, b


def _rglru_prompt_kernel(x_ref, gc_ref, cw_ref, cb_ref, wa_ref, wx_ref, ba_ref, bx_ref, lam_ref,
                         obuf_ref, o_ref, hf_ref, prev_ref, h_ref, *, Tc):
    del obuf_ref
    i = pl.program_id(1)

    @pl.when(i == 0)
    def _():
        prev_ref[...] = jnp.zeros(prev_ref.shape, F32)
        h_ref[...] = jnp.zeros(h_ref.shape, F32)

    x = x_ref[...]
    prev = prev_ref[...]
    t8 = lax.broadcasted_iota(jnp.int32, prev.shape, 0)
    cw = cw_ref[...]
    u = cb_ref[...] + x * cw[CONV_W - 1:CONV_W]
    for j in range(CONV_W - 1):
        d = CONV_W - 1 - j
        xs = pltpu.roll(x, d, 0)
        head = jnp.where(t8 >= d, xs[:SUBLANE], pltpu.roll(prev, d, 0))
        u = u + jnp.concatenate([head, xs[SUBLANE:]], axis=0) * cw[j:j + 1]
    prev_ref[...] = x[Tc - SUBLANE:]
    sp = jax.nn.softplus(-lam_ref[...])
    a, mult, iu = _rg_gates(u, wa_ref, wx_ref, ba_ref[...], bx_ref[...], sp)
    t = lax.broadcasted_iota(jnp.int32, a.shape, 0)
    mult = jnp.where((t == 0) & (i == 0), 1.0, mult)
    acc_a, acc_b = _scan_rows(a, mult * iu, t, int(math.log2(Tc)))
    h = acc_a * h_ref[0:1] + acc_b
    h_ref[...] = jnp.broadcast_to(h[Tc - 1:Tc], h_ref.shape)
    o_ref[...] = (h * jax.nn.gelu(gc_ref[...])).astype(BF16)

    @pl.when(i == pl.num_programs(1) - 1)
    def _():
        hf_ref[...] = h[Tc - 1:Tc]


def _rg_weight_specs(W, nidx):
    z = lambda shape: pl.BlockSpec(shape, lambda *_: (0,) * len(shape))
    return [z((CONV_W, W)), z((1, W)), z((C_BLOCKS, LANE, LANE)), z((C_BLOCKS, LANE, LANE)),
            z((1, W)), z((1, W)), z((1, W))]


def rglru_prompt(proj, wts, o_buf, B, S):
    W = C_BLOCKS * LANE
    Tc = _pick(S, (512, 256, 128))
    nt = S // Tc
    xcol = (3 * A_HEADS + 4 * B_HEADS) * LANE // W
    args = (proj, proj, *wts, o_buf)
    return pl.pallas_call(
        functools.partial(_rglru_prompt_kernel, Tc=Tc),
        grid=(B, nt),
        in_specs=[pl.BlockSpec((Tc, W), lambda b, i: (b * nt + i, xcol)),
                  pl.BlockSpec((Tc, W), lambda b, i: (b * nt + i, xcol + 1))] + _rg_weight_specs(W, 2)
                 + [pl.BlockSpec(memory_space=pl.ANY)],
        out_specs=[pl.BlockSpec((Tc, W), lambda b, i: (b * nt + i, 0)),
                   pl.BlockSpec((None, 1, W), lambda b, i: (b, 0, 0))],
        out_shape=[jax.ShapeDtypeStruct(o_buf.shape, BF16), jax.ShapeDtypeStruct((B, 1, W), F32)],
        input_output_aliases={len(args) - 1: 0},
        scratch_shapes=[pltpu.VMEM((SUBLANE, W), F32), pltpu.VMEM((SUBLANE, W), F32)],
        compiler_params=_cparams(("parallel", "arbitrary")),
        name="rglru_prompt",
    )(*args)


def _rglru_sample_kernel(x_ref, gc_ref, p_ref, h0_ref, cw_ref, cb_ref, wa_ref, wx_ref, ba_ref, bx_ref, lam_ref,
                         obuf_ref, o_ref, h_ref, *, Tq):
    del obuf_ref
    x = x_ref[...]
    n = x.shape[0]
    t = lax.broadcasted_iota(jnp.int32, x.shape, 0) % Tq
    p = p_ref[...]
    cw = cw_ref[...]
    u = cb_ref[...] + x * cw[CONV_W - 1:CONV_W]
    for j in range(CONV_W - 1):
        d = CONV_W - 1 - j
        shifted = jnp.where(t >= d, pltpu.roll(x, d, 0), pltpu.roll(p, n - (Tq - d), 0))
        u = u + shifted * cw[j:j + 1]
    sp = jax.nn.softplus(-lam_ref[...])
    a, mult, iu = _rg_gates(u, wa_ref, wx_ref, ba_ref[...], bx_ref[...], sp)
    acc_a, acc_b = _scan_rows(a, mult * iu, t, int(math.log2(Tq)))
    h = acc_a * h0_ref[...] + acc_b
    h_ref[...] = h
    o_ref[...] = (h * jax.nn.gelu(gc_ref[...])).astype(BF16)


def rglru_sample(proj, wts, pbuf, h0rep, o_buf, NP, NS, Tq):
    W = C_BLOCKS * LANE
    tr = _pick(NS, (256, 128, 64, 32, 16, 8))
    xcol = (3 * A_HEADS + 4 * B_HEADS) * LANE // W
    base = NP // tr
    loc = pl.BlockSpec((tr, W), lambda i: (i, 0))
    args = (proj, proj, pbuf, h0rep, *wts, o_buf)
    return pl.pallas_call(
        functools.partial(_rglru_sample_kernel, Tq=Tq),
        grid=(NS // tr,),
        in_specs=[pl.BlockSpec((tr, W), lambda i: (base + i, xcol)),
                  pl.BlockSpec((tr, W), lambda i: (base + i, xcol + 1)), loc, loc] + _rg_weight_specs(W, 1)
                 + [pl.BlockSpec(memory_space=pl.ANY)],
        out_specs=[pl.BlockSpec((tr, W), lambda i: (base + i, 0)), loc],
        out_shape=[jax.ShapeDtypeStruct(o_buf.shape, BF16), jax.ShapeDtypeStruct((NS, W), F32)],
        input_output_aliases={len(args) - 1: 0},
        compiler_params=_cparams(("parallel",)),
        name="rglru_sample",
    )(*args)


def _out_proj_kernel(x_ref, a_ref, b_ref, c_ref, wa_ref, wb_ref, wc_ref, o_ref):
    y = jnp.dot(a_ref[...], wa_ref[...], preferred_element_type=F32)
    y = y + jnp.dot(b_ref[...], wb_ref[...], preferred_element_type=F32)
    y = y + jnp.dot(c_ref[...], wc_ref[...], preferred_element_type=F32)
    o_ref[...] = x_ref[...] + y


def out_proj(x, oa, ob, oc, w):
    n, d = x.shape
    wa_, wb_, wc_ = oa.shape[1], ob.shape[1], oc.shape[1]
    assert wb_ == wc_ and wa_ % wb_ == 0
    tm = _pick(n, (1024, 512, 256, 128))
    tn = _pick(d, (512, 256, 128))
    return pl.pallas_call(
        _out_proj_kernel,
        grid=(n // tm, d // tn),
        in_specs=[pl.BlockSpec((tm, tn), lambda i, j: (i, j)),
                  pl.BlockSpec((tm, wa_), lambda i, j: (i, 0)),
                  pl.BlockSpec((tm, wb_), lambda i, j: (i, 0)),
                  pl.BlockSpec((tm, wc_), lambda i, j: (i, 0)),
                  pl.BlockSpec((wa_, tn), lambda i, j: (0, j)),
                  pl.BlockSpec((wb_, tn), lambda i, j: (wa_ // wb_, j)),
                  pl.BlockSpec((wc_, tn), lambda i, j: (wa_ // wb_ + 1, j))],
        out_specs=pl.BlockSpec((tm, tn), lambda i, j: (i, j)),
        out_shape=jax.ShapeDtypeStruct((n, d), F32),
        compiler_params=_cparams(("parallel", "arbitrary")),
        name="out_proj",
    )(x, oa, ob, oc, w, w, w)


def _ffn_up_kernel(te_ref, nu_ref, x_ref, g_ref, wg_ref, wu_ref, o_ref, h_ref, *, do_norm):
    i = pl.program_id(0)

    @pl.when(pl.program_id(1) == 0)
    def _():
        x = x_ref[...]
        if do_norm:
            x = _rms(x, g_ref[...])
        h_ref[...] = x.astype(BF16)

    @pl.when(i < nu_ref[0])
    def _():
        h = h_ref[...]
        a = jnp.dot(h, wg_ref[...], preferred_element_type=F32)
        b = jnp.dot(h, wu_ref[...], preferred_element_type=F32)
        o_ref[...] = (jax.nn.silu(a) * b).astype(BF16)

    @pl.when(i >= nu_ref[0])
    def _():
        o_ref[...] = jnp.zeros(o_ref.shape, BF16)


def ffn_up(x, g, wg, wu, te, nu, tm, do_norm):
    n, d = x.shape
    f = wg.shape[-1]
    tf = _pick(f, tuple(t for t in (1408, 512, 256, 128) if tm * t <= 512 * 1408))
    grid_spec = pltpu.PrefetchScalarGridSpec(
        num_scalar_prefetch=2,
        grid=(n // tm, f // tf),
        in_specs=[pl.BlockSpec((tm, d), lambda i, j, te, nu: (i, 0)),
                  pl.BlockSpec((1, d), lambda i, j, te, nu: (0, 0)),
                  pl.BlockSpec((None, d, tf), lambda i, j, te, nu: (te[i], 0, j)),
                  pl.BlockSpec((None, d, tf), lambda i, j, te, nu: (te[i], 0, j))],
        out_specs=pl.BlockSpec((tm, tf), lambda i, j, te, nu: (i, j)),
        scratch_shapes=[pltpu.VMEM((tm, d), BF16)])
    return pl.pallas_call(
        functools.partial(_ffn_up_kernel, do_norm=do_norm),
        grid_spec=grid_spec,
        out_shape=jax.ShapeDtypeStruct((n, f), BF16),
        compiler_params=_cparams(("parallel", "arbitrary")),
        name="ffn_up",
    )(te, nu, x, g.reshape(1, d), wg, wu)


def _expert_changed(te_ref, i):
    return (i == 0) | (te_ref[i] != te_ref[jnp.maximum(i - 1, 0)])


def _moe_up_kernel(te_ref, nu_ref, x_ref, wg_ref, wu_ref, o_ref, wgb_ref, wub_ref):
    i = pl.program_id(1)

    @pl.when(_expert_changed(te_ref, i))
    def _():
        wgb_ref[...] = wg_ref[...].astype(BF16)
        wub_ref[...] = wu_ref[...].astype(BF16)

    @pl.when(i < nu_ref[0])
    def _():
        x = x_ref[...]
        a = jnp.dot(x, wgb_ref[...], preferred_element_type=F32)
        b = jnp.dot(x, wub_ref[...], preferred_element_type=F32)
        o_ref[...] = (jax.nn.silu(a) * b).astype(BF16)

    @pl.when(i >= nu_ref[0])
    def _():
        o_ref[...] = jnp.zeros(o_ref.shape, BF16)


def moe_up(xs, wg, wu, m, te, nu, tm):
    n, d = xs.shape
    f = wg.shape[-1]
    tf = _pick(f, (512, 256, 128))
    wspec = pl.BlockSpec((None, None, d, tf), lambda j, i, te, nu: (m, te[i], 0, j))
    grid_spec = pltpu.PrefetchScalarGridSpec(
        num_scalar_prefetch=2,
        grid=(f // tf, n // tm),
        in_specs=[pl.BlockSpec((tm, d), lambda j, i, te, nu: (i, 0)), wspec, wspec],
        out_specs=pl.BlockSpec((tm, tf), lambda j, i, te, nu: (i, j)),
        scratch_shapes=[pltpu.VMEM((d, tf), BF16), pltpu.VMEM((d, tf), BF16)])
    return pl.pallas_call(
        _moe_up_kernel,
        grid_spec=grid_spec,
        out_shape=jax.ShapeDtypeStruct((n, f), BF16),
        compiler_params=_cparams(("arbitrary", "arbitrary")),
        name="moe_up",
    )(te, nu, xs, wg, wu)


def _moe_down_kernel(te_ref, nu_ref, a_ref, wd_ref, o_ref, wdb_ref):
    i = pl.program_id(1)

    @pl.when(_expert_changed(te_ref, i))
    def _():
        wdb_ref[...] = wd_ref[...].astype(BF16)

    @pl.when(i < nu_ref[0])
    def _():
        o_ref[...] = jnp.dot(a_ref[...], wdb_ref[...], preferred_element_type=F32)

    @pl.when(i >= nu_ref[0])
    def _():
        o_ref[...] = jnp.zeros(o_ref.shape, F32)


def moe_down(act, wd, m, te, nu, tm):
    n, f = act.shape
    d = wd.shape[-1]
    tn = _pick(d, (512, 256, 128))
    grid_spec = pltpu.PrefetchScalarGridSpec(
        num_scalar_prefetch=2,
        grid=(d // tn, n // tm),
        in_specs=[pl.BlockSpec((tm, f), lambda j, i, te, nu: (i, 0)),
                  pl.BlockSpec((None, None, f, tn), lambda j, i, te, nu: (m, te[i], 0, j))],
        out_specs=pl.BlockSpec((tm, tn), lambda j, i, te, nu: (i, j)),
        scratch_shapes=[pltpu.VMEM((f, tn), BF16)])
    return pl.pallas_call(
        _moe_down_kernel,
        grid_spec=grid_spec,
        out_shape=jax.ShapeDtypeStruct((n, d), F32),
        compiler_params=_cparams(("arbitrary", "arbitrary"), vmem_mb=56),
        name="moe_down",
    )(te, nu, act, wd)


def _ffn_down_kernel(te_ref, nu_ref, a_ref, wd_ref, *rest, residual):
    i = pl.program_id(0)
    if residual:
        r_ref, o_ref = rest
    else:
        (o_ref,) = rest

    @pl.when(i < nu_ref[0])
    def _():
        y = jnp.dot(a_ref[...], wd_ref[...], preferred_element_type=F32)
        if residual:
            y = r_ref[...] + y
        o_ref[...] = y

    @pl.when(i >= nu_ref[0])
    def _():
        o_ref[...] = jnp.zeros(o_ref.shape, F32)


def ffn_down(act, wd, te, nu, tm, res):
    n, f = act.shape
    d = wd.shape[-1]
    tn = _pick(d, tuple(t for t in (1024, 512, 256, 128) if tm * t <= 512 * 1024))
    in_specs = [pl.BlockSpec((tm, f), lambda i, j, te, nu: (i, 0)),
                pl.BlockSpec((None, f, tn), lambda i, j, te, nu: (te[i], 0, j))]
    args = [act, wd]
    if res is not None:
        in_specs.append(pl.BlockSpec((tm, tn), lambda i, j, te, nu: (i, j)))
        args.append(res)
    grid_spec = pltpu.PrefetchScalarGridSpec(
        num_scalar_prefetch=2,
        grid=(n // tm, d // tn),
        in_specs=in_specs,
        out_specs=pl.BlockSpec((tm, tn), lambda i, j, te, nu: (i, j)))
    return pl.pallas_call(
        functools.partial(_ffn_down_kernel, residual=res is not None),
        grid_spec=grid_spec,
        out_shape=jax.ShapeDtypeStruct((n, d), F32),
        compiler_params=_cparams(("parallel", "arbitrary")),
        name="ffn_down",
    )(te, nu, *args)


def _router_kernel(x_ref, g_ref, wr_ref, h_ref, rt_ref):
    h = _rms(x_ref[...], g_ref[...])
    h_ref[...] = h
    logits = jnp.dot(h, wr_ref[...], precision=HI, preferred_element_type=F32)
    lane = lax.broadcasted_iota(jnp.int32, logits.shape, 1)
    l1 = jnp.where(lane < N_EXPERTS, logits, -jnp.inf)
    m1 = jnp.max(l1, axis=-1, keepdims=True)
    i1 = jnp.min(jnp.where(l1 == m1, lane, LANE), axis=-1, keepdims=True)
    l2 = jnp.where(lane == i1, -jnp.inf, l1)
    m2 = jnp.max(l2, axis=-1, keepdims=True)
    i2 = jnp.min(jnp.where(l2 == m2, lane, LANE), axis=-1, keepdims=True)
    e = jnp.exp(m2 - m1)
    g1 = 1.0 / (1.0 + e)
    g2 = e / (1.0 + e)
    rt_ref[...] = jnp.where(lane == 0, i1.astype(F32),
                            jnp.where(lane == 1, i2.astype(F32),
                                      jnp.where(lane == 2, g1, jnp.where(lane == 3, g2, 0.0))))


def route_tokens(x, g, wr):
    n, d = x.shape
    tm = _pick(n, (512, 256, 128))
    wr_pad = jnp.zeros((d, LANE), F32).at[:, :N_EXPERTS].set(wr)
    return pl.pallas_call(
        _router_kernel,
        grid=(n // tm,),
        in_specs=[pl.BlockSpec((tm, d), lambda i: (i, 0)),
                  pl.BlockSpec((1, d), lambda i: (0, 0)),
                  pl.BlockSpec((d, LANE), lambda i: (0, 0))],
        out_specs=[pl.BlockSpec((tm, d), lambda i: (i, 0)), pl.BlockSpec((tm, LANE), lambda i: (i, 0))],
        out_shape=[jax.ShapeDtypeStruct((n, d), F32), jax.ShapeDtypeStruct((n, LANE), F32)],
        compiler_params=_cparams(("parallel",)),
        name="router",
    )(x, g.reshape(1, d), wr_pad)


DISPATCH_ROWS = 256
COMBINE_ROWS = 128
GATHER_UNROLL = 8


def _gather_start(idx_ref, idx0, src_ref, dst_ref, sem, rows):
    def issue(t, c):
        pltpu.make_async_copy(src_ref.at[pl.ds(idx_ref[idx0 + t], 1)], dst_ref.at[pl.ds(t, 1)], sem).start()
        return c
    lax.fori_loop(0, rows, issue, 0, unroll=GATHER_UNROLL)


def _gather_wait(src_ref, dst_ref, sem, rows):
    pltpu.make_async_copy(src_ref.at[pl.ds(0, rows)], dst_ref, sem).wait()


def _dispatch_kernel(src_ref, nrows_ref, h_ref, o_ref, buf_ref, sem):
    i = pl.program_id(0)
    R = DISPATCH_ROWS
    live = lambda s: s * R < nrows_ref[0]

    def start(s):
        _gather_start(src_ref, s * R, h_ref, buf_ref.at[s % 2], sem.at[s % 2], R)

    @pl.when(i == 0)
    def _():
        start(i)

    @pl.when((i + 1 < pl.num_programs(0)) & live(i + 1))
    def _():
        start(i + 1)

    @pl.when(live(i))
    def _():
        _gather_wait(h_ref, buf_ref.at[i % 2], sem.at[i % 2], R)
        o_ref[...] = buf_ref[i % 2].astype(BF16)

    @pl.when(jnp.logical_not(live(i)))
    def _():
        o_ref[...] = jnp.zeros(o_ref.shape, BF16)


def dispatch(h, src, nrows):
    n, d = h.shape
    p_rows = src.shape[0]
    assert p_rows % DISPATCH_ROWS == 0
    grid_spec = pltpu.PrefetchScalarGridSpec(
        num_scalar_prefetch=2,
        grid=(p_rows // DISPATCH_ROWS,),
        in_specs=[pl.BlockSpec(memory_space=pl.ANY)],
        out_specs=pl.BlockSpec((DISPATCH_ROWS, d), lambda i, src, nr: (i, 0)),
        scratch_shapes=[pltpu.VMEM((2, DISPATCH_ROWS, d), F32), pltpu.SemaphoreType.DMA((2,))])
    return pl.pallas_call(
        _dispatch_kernel,
        grid_spec=grid_spec,
        out_shape=jax.ShapeDtypeStruct((p_rows, d), BF16),
        compiler_params=_cparams(("arbitrary",)),
        name="moe_dispatch",
    )(src, nrows, h)


def _combine_kernel(pos_ref, x_ref, rt_ref, y_ref, o_ref, buf_ref, sem, *, n):
    i = pl.program_id(0)
    R = COMBINE_ROWS

    def start(s):
        for k in range(TOP_K):
            _gather_start(pos_ref, k * n + s * R, y_ref, buf_ref.at[s % 2, k], sem.at[s % 2], R)

    @pl.when(i == 0)
    def _():
        start(i)

    @pl.when(i + 1 < pl.num_programs(0))
    def _():
        start(i + 1)

    for k in range(TOP_K):
        _gather_wait(y_ref, buf_ref.at[i % 2, k], sem.at[i % 2], R)
    rt = rt_ref[...]
    o_ref[...] = x_ref[...] + rt[:, 2:3] * buf_ref[i % 2, 0] + rt[:, 3:4] * buf_ref[i % 2, 1]


def combine(x, rt, y, pos):
    n, d = x.shape
    assert n % COMBINE_ROWS == 0
    grid_spec = pltpu.PrefetchScalarGridSpec(
        num_scalar_prefetch=1,
        grid=(n // COMBINE_ROWS,),
        in_specs=[pl.BlockSpec((COMBINE_ROWS, d), lambda i, pos: (i, 0)),
                  pl.BlockSpec((COMBINE_ROWS, LANE), lambda i, pos: (i, 0)),
                  pl.BlockSpec(memory_space=pl.ANY)],
        out_specs=pl.BlockSpec((COMBINE_ROWS, d), lambda i, pos: (i, 0)),
        scratch_shapes=[pltpu.VMEM((2, TOP_K, COMBINE_ROWS, d), F32), pltpu.SemaphoreType.DMA((2,))])
    return pl.pallas_call(
        functools.partial(_combine_kernel, n=n),
        grid_spec=grid_spec,
        out_shape=jax.ShapeDtypeStruct((n, d), F32),
        compiler_params=_cparams(("arbitrary",)),
        name="moe_combine",
    )(pos, x, rt, y)


def _route_plan(rt, tm, n_tiles):
    n = rt.shape[0]
    e_flat = jnp.concatenate([rt[:, 0], rt[:, 1]]).astype(jnp.int32)
    onehot = (e_flat[:, None] == jnp.arange(N_EXPERTS)[None, :]).astype(jnp.int32)
    csum = jnp.cumsum(onehot, axis=0)
    rank = jnp.sum(csum * onehot, axis=1) - 1
    counts = csum[-1]
    padded = ((counts + tm - 1) // tm) * tm
    gend = jnp.cumsum(padded)
    gstart = gend - padded
    pos = jnp.sum(onehot * gstart[None, :], axis=1) + rank
    nu = (gend[-1] // tm).astype(jnp.int32).reshape(1)
    tile_start = jnp.arange(n_tiles, dtype=jnp.int32) * tm
    te = jnp.sum((tile_start[:, None] >= gend[None, :]).astype(jnp.int32), axis=1)
    last = jnp.max(jnp.where(counts > 0, jnp.arange(N_EXPERTS), 0))
    te = jnp.minimum(te, last).astype(jnp.int32)
    pos = pos.astype(jnp.int32)
    tok = jnp.arange(TOP_K * n, dtype=jnp.int32) % n
    src = jnp.zeros((n_tiles * tm,), jnp.int32).at[pos].set(tok)
    return pos, src, te, nu


def ffn_moe(x, g, wr, wg, wu, wd, m):
    n, d = x.shape
    tm = _pick(n, (512, 256, 128))
    n_tiles = (TOP_K * n) // tm + N_EXPERTS
    h, rt = route_tokens(x, g, wr)
    pos, src, te, nu = _route_plan(rt, tm, n_tiles)
    xs = dispatch(h, src, nu * tm)
    act = moe_up(xs, wg, wu, m, te, nu, tm)
    y = moe_down(act, wd, m, te, nu, tm)
    return combine(x, rt, y, pos)


def ffn_dense(x, g, wg, wu, wd):
    n, d = x.shape
    tm = _pick(n, (1024, 512, 256, 128))
    te = jnp.zeros((n // tm,), jnp.int32)
    nu = jnp.full((1,), n // tm, jnp.int32)
    act = ffn_up(x, g, wg[None], wu[None], te, nu, tm, do_norm=True)
    return ffn_down(act, wd[None], te, nu, tm, x)


def kernel(x_prompt, x_sample, cache_k, cache_v, page_table, state_hgrn, state_rglru, state_conv,
           w_in, w_out, g_mix, g_ffn, q_norm, k_norm, lam_q1, lam_k1, lam_q2, lam_k2, subln,
           rel_table, hgrn_lb, hgrn_onorm, conv_w, conv_b, rg_wa, rg_ba, rg_wx, rg_bx, rg_lambda,
           ff_gate, ff_up, ff_down, router, ex_gate, ex_up, ex_down):
    B, S, D = x_prompt.shape
    DB, Tq, _ = x_sample.shape
    depth = w_in.shape[0]
    NP, NS = B * S, DB * Tq
    AW = A_HEADS * LANE
    CW = C_BLOCKS * LANE
    xc0 = 3 * AW + 4 * B_HEADS * LANE
    x = jnp.concatenate([x_prompt.reshape(NP, D), x_sample.reshape(NS, D)], axis=0)

    sm = jax.nn.softmax(hgrn_lb.astype(F32), axis=0)
    lb_all = jnp.cumsum(sm, axis=0) - sm[0:1]

    outs = {k: [] for k in ("kp", "vp", "sp", "hp", "bp", "ks", "vs", "ss", "hs", "bs")}
    for l in range(depth):
        lam_init = 0.8 - 0.6 * math.exp(-0.3 * l)
        lamp = jnp.stack([lam_q1[l], lam_k1[l], lam_q2[l], lam_k2[l]]).astype(F32)
        proj = norm_matmul(x, g_mix[l], w_in[l].astype(BF16))
        ta = _pick(math.gcd(S, NP + NS), (512, 256, 128))
        qf, kf, kb, qt, vt = qk_prep(proj, q_norm[l], k_norm[l], ta)
        vf = proj[:, 2 * AW:3 * AW]
        oa = attn_prompt(qt, kb, vt, rel_table, lamp, subln[l], jnp.zeros((NP + NS, AW), BF16), B, S, ta, lam_init)
        kn2 = kf[NP:].reshape(DB, Tq * A_HEADS, LANE)
        vn2 = vf[NP:].reshape(DB, Tq * A_HEADS, LANE)
        oa = attn_sample(qf, kn2, vn2, cache_k, cache_v, l, page_table, rel_table, lamp, subln[l], oa,
                         NP, DB, Tq, lam_init)
        ob, s_p = hgrn(proj, lb_all[l], hgrn_onorm[l], 0, B, S, None, l, jnp.zeros((NP + NS, B_HEADS * LANE), BF16))
        ob, s_s = hgrn(proj, lb_all[l], hgrn_onorm[l], NP, DB, Tq, state_hgrn, l, ob)
        wts = (conv_w[l], conv_b[l].reshape(1, CW), rg_wa[l], rg_wx[l], rg_ba[l].reshape(1, CW),
               rg_bx[l].reshape(1, CW), rg_lambda[l].reshape(1, CW))
        oc, h_p = rglru_prompt(proj, wts, jnp.zeros((NP + NS, CW), BF16), B, S)
        pbuf = jnp.concatenate([jnp.zeros((DB, Tq - (CONV_W - 1), CW), F32), state_conv[l]], axis=1).reshape(NS, CW)
        h0rep = jnp.repeat(state_rglru[l], Tq, axis=0)
        oc, h_s = rglru_sample(proj, wts, pbuf, h0rep, oc, NP, NS, Tq)
        x = out_proj(x, oa, ob, oc, w_out[l].astype(BF16))
        m = l // 2
        if l % 2 == 0:
            x = ffn_dense(x, g_ffn[l], ff_gate[m].astype(BF16), ff_up[m].astype(BF16), ff_down[m].astype(BF16))
        else:
            x = ffn_moe(x, g_ffn[l], router[m], ex_gate, ex_up, ex_down, m)
        xcs = proj[:, xc0:xc0 + CW]
        outs["kp"].append(kf[:NP].reshape(B, S, A_HEADS, LANE))
        outs["vp"].append(vf[:NP].reshape(B, S, A_HEADS, LANE))
        outs["sp"].append(s_p)
        outs["hp"].append(h_p.reshape(B, CW))
        outs["bp"].append(xcs[:NP].reshape(B, S, CW)[:, S - (CONV_W - 1):])
        outs["ks"].append(kf[NP:].reshape(DB, Tq, A_HEADS, LANE))
        outs["vs"].append(vf[NP:].reshape(DB, Tq, A_HEADS, LANE))
        outs["ss"].append(s_s)
        outs["hs"].append(h_s.reshape(DB, Tq, CW)[:, Tq - 1])
        outs["bs"].append(xcs[NP:].reshape(DB, Tq, CW)[:, Tq - (CONV_W - 1):])
    st = {k: jnp.stack(v) for k, v in outs.items()}
    return (x[:NP].reshape(B, S, D), x[NP:].reshape(DB, Tq, D),
            st["kp"], st["vp"], st["sp"], st["hp"], st["bp"],
            st["ks"], st["vs"], st["ss"], st["hs"], st["bs"])
```

```python
import functools
import math

import jax
import jax.numpy as jnp
from jax import lax
from jax.experimental import pallas as pl
from jax.experimental.pallas import tpu as pltpu

F32 = jnp.float32
BF16 = jnp.bfloat16
HI = lax.Precision.HIGHEST
EPS = 1e-6
NEG = -1e30
LANE = 128
SUBLANE = 8
NT = (((1,), (1,)), ((), ()))
TN = (((0,), (0,)), ((), ()))

A_HEADS = 8
A_SUB = 64
A_VDIM = 128
A_SCALE = A_SUB ** -0.5
LOG2E = math.log2(math.e)
Q_SCALE = A_SCALE * LOG2E
B_HEADS = 4
C_BLOCKS = 4
CONV_W = 4
RG_C = 8.0
PAGE = 128
N_BUCKETS = 32
MAX_EXACT = 16
MAX_DISTANCE = 128
N_EXPERTS = 8
TOP_K = 2


def _cparams(sem, vmem_mb=48):
    return pltpu.CompilerParams(dimension_semantics=sem, vmem_limit_bytes=vmem_mb << 20)


def _pick(n, prefs):
    for p in prefs:
        if n % p == 0:
            return p
    raise ValueError(f"no tile in {prefs} divides {n}")


def _rms(x, g):
    return x * lax.rsqrt(jnp.mean(x * x, axis=-1, keepdims=True) + EPS) * g


def _norm_matmul_kernel(x_ref, g_ref, w_ref, o_ref, h_ref):
    @pl.when(pl.program_id(1) == 0)
    def _():
        h_ref[...] = _rms(x_ref[...], g_ref[...]).astype(BF16)

    o_ref[...] = jnp.dot(h_ref[...], w_ref[...], preferred_element_type=F32)


def norm_matmul(x, g, w):
    n, d = x.shape
    c = w.shape[1]
    tm = _pick(n, (1024, 512, 256, 128))
    tn = _pick(c, (1024, 512, 256, 128))
    return pl.pallas_call(
        _norm_matmul_kernel,
        grid=(n // tm, c // tn),
        in_specs=[pl.BlockSpec((tm, d), lambda i, j: (i, 0)),
                  pl.BlockSpec((1, d), lambda i, j: (0, 0)),
                  pl.BlockSpec((d, tn), lambda i, j: (0, j))],
        out_specs=pl.BlockSpec((tm, tn), lambda i, j: (i, j)),
        out_shape=jax.ShapeDtypeStruct((n, c), F32),
        scratch_shapes=[pltpu.VMEM((tm, d), BF16)],
        compiler_params=_cparams(("parallel", "arbitrary")),
        name="norm_matmul",
    )(x, g.reshape(1, d), w)


def _qk_prep_kernel(q_ref, k_ref, v_ref, qg_ref, kg_ref, qf_ref, kf_ref, kb_ref, qt_ref, vt_ref):
    r = lax.broadcasted_iota(jnp.int32, (LANE, LANE), 0)
    c = lax.broadcasted_iota(jnp.int32, (LANE, LANE), 1)
    seg = jnp.where((r // A_SUB) == (c // A_SUB), 1.0 / A_SUB, 0.0).astype(F32)

    def normed(src, g_ref, sl):
        x = src[:, sl]
        ms = jnp.dot(x * x, seg, precision=HI, preferred_element_type=F32)
        return x * lax.rsqrt(ms + EPS) * g_ref[...]

    for h in range(A_HEADS):
        sl = slice(h * LANE, (h + 1) * LANE)
        q = normed(q_ref, qg_ref, sl) * Q_SCALE
        qf_ref[:, sl] = q
        qt_ref[h] = q.T.astype(BF16)
        k = normed(k_ref, kg_ref, sl)
        kf_ref[:, sl] = k
        kb_ref[:, sl] = k.astype(BF16)
        vt_ref[h] = v_ref[:, sl].T.astype(BF16)


def qk_prep(proj, q_norm, k_norm, tm):
    n = proj.shape[0]
    w = A_HEADS * LANE
    qg = jnp.concatenate([q_norm, q_norm]).reshape(1, LANE)
    kg = jnp.concatenate([k_norm, k_norm]).reshape(1, LANE)
    col = lambda cb: pl.BlockSpec((tm, w), lambda i, cb=cb: (i, cb))
    vec = pl.BlockSpec((1, LANE), lambda i: (0, 0))
    out = pl.BlockSpec((tm, w), lambda i: (i, 0))
    out_t = pl.BlockSpec((A_HEADS, None, LANE, tm), lambda i: (0, i, 0, 0))
    t_shape = jax.ShapeDtypeStruct((A_HEADS, n // tm, LANE, tm), BF16)
    return pl.pallas_call(
        _qk_prep_kernel,
        grid=(n // tm,),
        in_specs=[col(0), col(1), col(2), vec, vec],
        out_specs=[out, out, out, out_t, out_t],
        out_shape=[jax.ShapeDtypeStruct((n, w), F32)] * 2 + [jax.ShapeDtypeStruct((n, w), BF16), t_shape, t_shape],
        compiler_params=_cparams(("parallel",)),
        name="qk_prep",
    )(proj, proj, proj, qg, kg)


def _rel_bias(table, n):
    nf = jnp.maximum(n, 1).astype(F32)
    far = MAX_EXACT + (jnp.log(nf / MAX_EXACT) / math.log(MAX_DISTANCE / MAX_EXACT)
                       * (N_BUCKETS - MAX_EXACT)).astype(jnp.int32)
    bucket = jnp.where(n < MAX_EXACT, n, jnp.minimum(far, N_BUCKETS - 1))
    onehot = (bucket.reshape(1, -1) == jnp.arange(N_BUCKETS)[:, None]).astype(F32)
    t = table.astype(F32)
    vals = jnp.dot(t.T, onehot, precision=HI)
    return ((vals - t[N_BUCKETS - 1][:, None]) * LOG2E).reshape((t.shape[1],) + n.shape)


def _lam_from(lamp, lam_init):
    s1 = jnp.sum(lamp[0:1] * lamp[1:2], axis=-1, keepdims=True)
    s2 = jnp.sum(lamp[2:3] * lamp[3:4], axis=-1, keepdims=True)
    return jnp.exp(s1) - jnp.exp(s2) + lam_init


def _attn_prompt_kernel(q_ref, k_ref, v_ref, bias_ref, lamp_ref, subln_ref, obuf_ref, o_ref,
                        m_ref, l_ref, acc_ref, *, T, lam_init):
    del obuf_ref
    qi = pl.program_id(2)
    qt = q_ref[...]
    sub = lax.broadcasted_iota(jnp.int32, qt.shape, 0)
    zero = jnp.zeros_like(qt)
    qs = (jnp.where(sub < A_SUB, qt, zero), jnp.where(sub >= A_SUB, qt, zero))
    m_ref[...] = jnp.full(m_ref.shape, NEG, F32)
    l_ref[...] = jnp.zeros(l_ref.shape, F32)
    acc_ref[...] = jnp.zeros(acc_ref.shape, F32)

    def tile(kt, bias, width=1):
        k = k_ref[pl.ds(pl.multiple_of(kt * T, T), width * T), :]
        vts = [v_ref[kt + w] for w in range(width)]
        for c in range(2):
            s = jnp.dot(k, qs[c], preferred_element_type=F32)
            if bias is not None:
                s = s + bias
            m_old = m_ref[c]
            m_new = jnp.maximum(m_old, jnp.max(s, axis=0, keepdims=True))
            alpha = jnp.exp2(m_old - m_new)
            p = jnp.exp2(s - m_new)
            l_ref[c] = alpha * l_ref[c] + jnp.sum(p, axis=0, keepdims=True)
            pb = p.astype(BF16)
            acc = alpha * acc_ref[c]
            for w in range(width):
                acc = acc + jnp.dot(vts[w], pb[w * T:(w + 1) * T], preferred_element_type=F32)
            acc_ref[c] = acc
            m_ref[c] = m_new

    n_far = jnp.maximum(qi - 1, 0)

    def far_pair(pr, carry):
        tile(2 * pr, None, width=2)
        return carry

    lax.fori_loop(0, n_far // 2, far_pair, 0)

    @pl.when(n_far % 2 == 1)
    def _():
        tile(n_far - 1, None)

    nb = T // LANE
    same, after = bias_ref[0], bias_ref[1]
    zero_b = jnp.zeros((LANE, LANE), F32)
    neg_b = jnp.full((LANE, LANE), NEG, F32)
    grid_of = lambda pick: jnp.concatenate(
        [jnp.concatenate([pick(jk, iq) for iq in range(nb)], axis=1) for jk in range(nb)], axis=0)

    @pl.when(qi >= 1)
    def _():
        tile(qi - 1, grid_of(lambda jk, iq: after if (jk == nb - 1 and iq == 0) else zero_b))

    tile(qi, grid_of(lambda jk, iq: neg_b if iq < jk else same if iq == jk else after if iq == jk + 1 else zero_b))

    lam = _lam_from(lamp_ref[...], lam_init)
    ot = acc_ref[0] / l_ref[0] - lam * (acc_ref[1] / l_ref[1])
    ms = jnp.mean(ot * ot, axis=0, keepdims=True)
    y = ot * lax.rsqrt(ms + EPS) * subln_ref[...] * (1.0 - lam_init)
    o_ref[...] = y.T.astype(BF16)


def attn_prompt(qt, kb, vt, rel_table, lamp, subln, o_buf, B, S, T, lam_init):
    assert T >= MAX_DISTANCE and S % T == 0
    nq = S // T
    assert LANE == MAX_DISTANCE
    d = jnp.arange(LANE)[None, :] - jnp.arange(LANE)[:, None]
    bias = jnp.stack([jnp.where(d >= 0, _rel_bias(rel_table, jnp.maximum(d, 0)), NEG),
                      _rel_bias(rel_table, d + LANE)], axis=1)
    return pl.pallas_call(
        functools.partial(_attn_prompt_kernel, T=T, lam_init=lam_init),
        grid=(B, A_HEADS, nq),
        in_specs=[pl.BlockSpec((None, None, LANE, T), lambda b, h, i: (h, b * nq + i, 0, 0)),
                  pl.BlockSpec((S, LANE), lambda b, h, i: (b, h)),
                  pl.BlockSpec((None, nq, LANE, T), lambda b, h, i: (h, b, 0, 0)),
                  pl.BlockSpec((None, 2, LANE, LANE), lambda b, h, i: (h, 0, 0, 0)),
                  pl.BlockSpec((4, A_SUB), lambda b, h, i: (0, 0)),
                  pl.BlockSpec((LANE, 1), lambda b, h, i: (0, 0)),
                  pl.BlockSpec(memory_space=pl.ANY)],
        out_specs=pl.BlockSpec((T, LANE), lambda b, h, i: (b * nq + i, h)),
        out_shape=jax.ShapeDtypeStruct(o_buf.shape, BF16),
        input_output_aliases={6: 0},
        scratch_shapes=[pltpu.VMEM((2, 1, T), F32), pltpu.VMEM((2, 1, T), F32),
                        pltpu.VMEM((2, LANE, T), F32)],
        compiler_params=_cparams(("parallel", "parallel", "arbitrary")),
        name="attn_prompt",
    )(qt, kb, vt, bias, lamp, subln.reshape(LANE, 1), o_buf)


def _attn_sample_kernel(pt_ref, q_ref, kn_ref, vn_ref, mask_ref, bnew_ref, lamp_ref, subln_ref, *rest,
                        PPS, n_steps, lam_init):
    k_refs = rest[:PPS]
    v_refs = rest[PPS:2 * PPS]
    o_ref = rest[2 * PPS + 1]
    qm_ref, m_ref, l_ref, acc_ref, s_ref = rest[2 * PPS + 2:]
    L = PAGE * A_HEADS
    step = pl.program_id(1)
    Tq = q_ref.shape[0]

    @pl.when(step == 0)
    def _():
        q = q_ref[...]
        lane = lax.broadcasted_iota(jnp.int32, (Tq, LANE), 1)
        rows = []
        for h in range(A_HEADS):
            qh = q[:, h * LANE:(h + 1) * LANE]
            rows.append(jnp.where(lane < A_SUB, qh, 0.0))
            rows.append(jnp.where(lane >= A_SUB, qh, 0.0))
        qm_ref[...] = jnp.concatenate(rows, axis=0).astype(BF16)
        m_ref[...] = jnp.full(m_ref.shape, NEG, F32)
        l_ref[...] = jnp.zeros(l_ref.shape, F32)
        acc_ref[...] = jnp.zeros(acc_ref.shape, F32)

    qm = qm_ref[...]
    mel = None
    for j in range(PPS):
        k2 = k_refs[j][...].reshape(L, LANE).astype(BF16)
        s = lax.dot_general(qm, k2, NT, preferred_element_type=F32)
        if j == PPS - 1:
            s = s + mask_ref[jnp.where(step == n_steps - 1, 1, 0)]
        else:
            s = s + mask_ref[0]
        s_ref[:, j * L:(j + 1) * L] = s
        mel = s if mel is None else jnp.maximum(mel, s)
    m_old = m_ref[...]
    m_new = jnp.maximum(m_old, jnp.max(mel, axis=-1, keepdims=True))
    alpha = jnp.exp2(m_old - m_new)
    lsum = None
    acc = alpha * acc_ref[...]
    for j in range(PPS):
        p = jnp.exp2(s_ref[:, j * L:(j + 1) * L] - m_new)
        lsum = p if lsum is None else lsum + p
        v2 = v_refs[j][...].reshape(L, LANE).astype(BF16)
        acc = acc + jnp.dot(p.astype(BF16), v2, preferred_element_type=F32)
    l_ref[...] = alpha * l_ref[...] + jnp.sum(lsum, axis=-1, keepdims=True)
    acc_ref[...] = acc
    m_ref[...] = m_new

    @pl.when(step == n_steps - 1)
    def _():
        kn2 = kn_ref[...].astype(BF16)
        vn2 = vn_ref[...].astype(BF16)
        s = lax.dot_general(qm, kn2, NT, preferred_element_type=F32) + bnew_ref[...]
        m_fin = jnp.maximum(m_new, jnp.max(s, axis=-1, keepdims=True))
        a_fin = jnp.exp2(m_new - m_fin)
        p = jnp.exp2(s - m_fin)
        l_ref[...] = a_fin * l_ref[...] + jnp.sum(p, axis=-1, keepdims=True)
        acc_ref[...] = a_fin * acc_ref[...] + jnp.dot(p.astype(BF16), vn2, preferred_element_type=F32)
        lam = _lam_from(lamp_ref[...], lam_init)
        o = acc_ref[...] / l_ref[...]
        for h in range(A_HEADS):
            r0 = h * 2 * Tq
            oh = o[r0:r0 + Tq] - lam * o[r0 + Tq:r0 + 2 * Tq]
            o_ref[:, h * LANE:(h + 1) * LANE] = (_rms(oh, subln_ref[...]) * (1.0 - lam_init)).astype(BF16)


def attn_sample(qf, kn2, vn2, cache_k, cache_v, layer, page_table, rel_table, lamp, subln, o_buf,
                NP, DB, Tq, lam_init):
    n_pages = page_table.shape[1]
    PPS = _pick(n_pages, (16, 8, 4, 2, 1))
    n_steps = n_pages // PPS
    assert Tq <= PAGE and PAGE >= MAX_DISTANCE
    R = A_HEADS * 2 * Tq
    L = PAGE * A_HEADS
    t_ = jnp.arange(Tq)[:, None]
    b_last = _rel_bias(rel_table, PAGE + t_ - jnp.arange(PAGE)[None, :])
    d_new = t_ - jnp.arange(Tq)[None, :]
    b_new = jnp.where(d_new >= 0, _rel_bias(rel_table, jnp.maximum(d_new, 0)), NEG)
    same = (jnp.arange(A_HEADS)[:, None] == jnp.arange(A_HEADS)[None, :])[:, None, None, None, :]
    expand = lambda b, nk: jnp.where(same, jnp.broadcast_to(b[:, None, :, :, None], (A_HEADS, 2, Tq, nk, A_HEADS)),
                                     NEG).reshape(R, nk * A_HEADS)
    masks = jnp.stack([expand(jnp.zeros((A_HEADS, Tq, PAGE), F32), PAGE), expand(b_last, PAGE)])
    bnew = expand(b_new, Tq)

    pt_flat = page_table.reshape(-1).astype(jnp.int32)
    page_spec = lambda j: pl.BlockSpec(
        (None, None, PAGE, A_HEADS, LANE),
        lambda b, s, pt, j=j: (layer, pt[b * n_pages + s * PPS + j], 0, 0, 0))
    const2 = lambda shape: pl.BlockSpec(shape, lambda b, s, pt: (0,) * len(shape))
    in_specs = ([pl.BlockSpec((Tq, A_HEADS * LANE), lambda b, s, pt: (NP // Tq + b, 0)),
                 pl.BlockSpec((None, Tq * A_HEADS, LANE), lambda b, s, pt: (b, 0, 0)),
                 pl.BlockSpec((None, Tq * A_HEADS, LANE), lambda b, s, pt: (b, 0, 0)),
                 const2((2, R, L)), const2((R, Tq * A_HEADS)), const2((4, A_SUB)), const2((1, LANE))]
                + [page_spec(j) for j in range(PPS)] * 2 + [pl.BlockSpec(memory_space=pl.ANY)])
    grid_spec = pltpu.PrefetchScalarGridSpec(
        num_scalar_prefetch=1,
        grid=(DB, n_steps),
        in_specs=in_specs,
        out_specs=pl.BlockSpec((Tq, A_HEADS * LANE), lambda b, s, pt: (NP // Tq + b, 0)),
        scratch_shapes=[pltpu.VMEM((R, LANE), BF16), pltpu.VMEM((R, 1), F32), pltpu.VMEM((R, 1), F32),
                        pltpu.VMEM((R, LANE), F32), pltpu.VMEM((R, PPS * L), F32)])
    return pl.pallas_call(
        functools.partial(_attn_sample_kernel, PPS=PPS, n_steps=n_steps, lam_init=lam_init),
        grid_spec=grid_spec,
        out_shape=jax.ShapeDtypeStruct(o_buf.shape, BF16),
        input_output_aliases={8 + 2 * PPS: 0},
        compiler_params=_cparams(("parallel", "arbitrary"), vmem_mb=56),
        name="attn_sample",
    )(pt_flat, qf, kn2, vn2, masks, bnew, lamp, subln.reshape(1, LANE),
      *([cache_k] * PPS), *([cache_v] * PPS), o_buf)


HG_C = 128


def _hgrn_masks(levels, R):
    row = jnp.arange(R)[:, None]
    col = jnp.arange(HG_C)[None, :]
    blocks = [row >= col]
    for lv in range(levels):
        s = 1 << lv
        mid = (row // (2 * s)) * (2 * s) + (s - 1)
        second = (row % (2 * s)) >= s
        blocks.append(jnp.where(second, (col > mid) & (col <= row), (col > row) & (col <= mid)))
    return jnp.concatenate(blocks, axis=0).astype(BF16)


def _hgrn_kernel(*refs, R, levels, has_state):
    if has_state:
        q_ref, f_ref, i_ref, g_ref, lb_ref, on_ref, mk_ref, s0_ref, _, o_ref, sf_ref, S_ref = refs
    else:
        q_ref, f_ref, i_ref, g_ref, lb_ref, on_ref, mk_ref, _, o_ref, sf_ref, S_ref = refs
    c = pl.program_id(1)

    @pl.when(c == 0)
    def _():
        S_ref[...] = s0_ref[...] if has_state else jnp.zeros(S_ref.shape, F32)

    row = lax.broadcasted_iota(jnp.int32, (R, HG_C), 0)
    col = lax.broadcasted_iota(jnp.int32, (R, HG_C), 1)
    eye = (lax.broadcasted_iota(jnp.int32, (LANE, LANE), 0) == lax.broadcasted_iota(jnp.int32, (LANE, LANE), 1))
    mk = mk_ref[...]
    dot = functools.partial(jnp.dot, preferred_element_type=F32)

    def pad_bf16(x):
        if R < HG_C:
            x = jnp.concatenate([x, jnp.zeros((HG_C - R, x.shape[1]), F32)], axis=0)
        return x.astype(BF16)

    for h in range(B_HEADS):
        sl = slice(h * LANE, (h + 1) * LANE)
        lb = lb_ref[:, sl]
        f = lb + (1.0 - lb) * jax.nn.sigmoid(f_ref[:, sl])
        g = jnp.log(f)
        kk = 1.0 - f
        qq = jax.nn.silu(q_ref[:, sl])
        v = pad_bf16(i_ref[:, sl])
        g0 = g.astype(BF16).astype(F32)
        r1 = g - g0
        g1 = r1.astype(BF16).astype(F32)
        e3 = dot(mk, pad_bf16(jnp.concatenate([g0, g1, r1 - g1], axis=-1)))
        e = e3[:, :LANE] + e3[:, LANE:2 * LANE] + e3[:, 2 * LANE:]
        b = e[:R]
        A = jnp.where(row == col, jnp.sum(qq * kk, axis=-1, keepdims=True), 0.0)
        for lv in range(levels):
            s = 1 << lv
            x = jnp.exp(e[(lv + 1) * R:(lv + 2) * R])
            second = (row % (2 * s)) >= s
            ql = jnp.where(second, qq * x, 0.0).astype(BF16)
            kl = pad_bf16(jnp.where(second, 0.0, kk * x))
            al = lax.dot_general(ql, kl, NT, preferred_element_type=F32)
            A = A + jnp.where((row // (2 * s)) == (col // (2 * s)), al, 0.0)
        S = S_ref[h]
        o = dot((qq * jnp.exp(b)).astype(BF16), S.astype(BF16)) + dot(A.astype(BF16), v)
        b_last = b[R - 1:R]
        e_col = jnp.sum(jnp.where(eye, jnp.exp(b_last), 0.0), axis=1, keepdims=True)
        kd = pad_bf16(kk * jnp.exp(b_last - b))
        S_new = e_col * S + lax.dot_general(kd, v, TN, preferred_element_type=F32)
        S_ref[h] = S_new
        o_ref[:, sl] = (_rms(o, on_ref[...]) * jax.nn.silu(g_ref[:, sl])).astype(BF16)

        @pl.when(c == pl.num_programs(1) - 1)
        def _():
            sf_ref[h] = S_new


def hgrn(proj, lb, onorm, row0, nb, T, s0, layer, o_buf):
    R = min(T, HG_C)
    nc = T // R
    levels = int(math.log2(R))
    base = row0 // R
    BW = B_HEADS * LANE
    qcol = 3 * A_HEADS * LANE // BW
    mk = _hgrn_masks(levels, R)
    blk = lambda cb: pl.BlockSpec((R, BW), lambda b, c, cb=cb: (base + b * nc + c, cb))
    in_specs = [blk(qcol), blk(qcol + 1), blk(qcol + 2), blk(qcol + 3),
                pl.BlockSpec((1, BW), lambda b, c: (0, 0)),
                pl.BlockSpec((1, LANE), lambda b, c: (0, 0)),
                pl.BlockSpec(mk.shape, lambda b, c: (0, 0))]
    args = [proj, proj, proj, proj, lb.reshape(1, BW), onorm.reshape(1, LANE), mk]
    if s0 is not None:
        in_specs.append(pl.BlockSpec((None, None, B_HEADS, LANE, LANE), lambda b, c: (layer, b, 0, 0, 0)))
        args.append(s0)
    in_specs.append(pl.BlockSpec(memory_space=pl.ANY))
    args.append(o_buf)
    return pl.pallas_call(
        functools.partial(_hgrn_kernel, R=R, levels=levels, has_state=s0 is not None),
        grid=(nb, nc),
        in_specs=in_specs,
        out_specs=[pl.BlockSpec((R, BW), lambda b, c: (base + b * nc + c, 0)),
                   pl.BlockSpec((None, B_HEADS, LANE, LANE), lambda b, c: (b, 0, 0, 0))],
        out_shape=[jax.ShapeDtypeStruct(o_buf.shape, BF16),
                   jax.ShapeDtypeStruct((nb, B_HEADS, LANE, LANE), F32)],
        input_output_aliases={len(args) - 1: 0},
        scratch_shapes=[pltpu.VMEM((B_HEADS, LANE, LANE), F32)],
        compiler_params=_cparams(("parallel", "arbitrary")),
        name="hgrn",
    )(*args)


def _rg_gates(u, wa_ref, wx_ref, ba, bx, sp):
    ra, rx = [], []
    for n in range(C_BLOCKS):
        un = u[:, n * LANE:(n + 1) * LANE]
        ra.append(jnp.dot(un, wa_ref[n], precision=HI, preferred_element_type=F32))
        rx.append(jnp.dot(un, wx_ref[n], precision=HI, preferred_element_type=F32))
    r = jax.nn.sigmoid(jnp.concatenate(ra, axis=-1) + ba)
    ig = jax.nn.sigmoid(jnp.concatenate(rx, axis=-1) + bx)
    log_a = -RG_C * r * sp
    a = jnp.exp(log_a)
    z = 2.0 * log_a
    u2 = jnp.exp(z)
    em1 = jnp.where(u2 == 1.0, z, (u2 - 1.0) * z / jnp.log(u2))
    mult = jnp.sqrt(-em1)
    return a, mult, ig * u


def _scan_rows(a, b, t, n_steps):
    n = a.shape[0]
    for lv in range(n_steps):
        d = 1 << lv
        ok = t >= d
        a_s = jnp.where(ok, pltpu.roll(a, d, 0), 1.0)
        b_s = jnp.where(ok, pltpu.roll(b, d, 0), 0.0)
        b = a * b_s + b
        a = a * a_s
    return a, b


def _rglru_prompt_kernel(x_ref, gc_ref, cw_ref, cb_ref, wa_ref, wx_ref, ba_ref, bx_ref, lam_ref,
                         obuf_ref, o_ref, hf_ref, prev_ref, h_ref, *, Tc):
    del obuf_ref
    i = pl.program_id(1)

    @pl.when(i == 0)
    def _():
        prev_ref[...] = jnp.zeros(prev_ref.shape, F32)
        h_ref[...] = jnp.zeros(h_ref.shape, F32)

    x = x_ref[...]
    prev = prev_ref[...]
    t8 = lax.broadcasted_iota(jnp.int32, prev.shape, 0)
    cw = cw_ref[...]
    u = cb_ref[...] + x * cw[CONV_W - 1:CONV_W]
    for j in range(CONV_W - 1):
        d = CONV_W - 1 - j
        xs = pltpu.roll(x, d, 0)
        head = jnp.where(t8 >= d, xs[:SUBLANE], pltpu.roll(prev, d, 0))
        u = u + jnp.concatenate([head, xs[SUBLANE:]], axis=0) * cw[j:j + 1]
    prev_ref[...] = x[Tc - SUBLANE:]
    sp = jax.nn.softplus(-lam_ref[...])
    a, mult, iu = _rg_gates(u, wa_ref, wx_ref, ba_ref[...], bx_ref[...], sp)
    t = lax.broadcasted_iota(jnp.int32, a.shape, 0)
    mult = jnp.where((t == 0) & (i == 0), 1.0, mult)
    acc_a, acc_b = _scan_rows(a, mult * iu, t, int(math.log2(Tc)))
    h = acc_a * h_ref[0:1] + acc_b
    h_ref[...] = jnp.broadcast_to(h[Tc - 1:Tc], h_ref.shape)
    o_ref[...] = (h * jax.nn.gelu(gc_ref[...])).astype(BF16)

    @pl.when(i == pl.num_programs(1) - 1)
    def _():
        hf_ref[...] = h[Tc - 1:Tc]


def _rg_weight_specs(W, nidx):
    z = lambda shape: pl.BlockSpec(shape, lambda *_: (0,) * len(shape))
    return [z((CONV_W, W)), z((1, W)), z((C_BLOCKS, LANE, LANE)), z((C_BLOCKS, LANE, LANE)),
            z((1, W)), z((1, W)), z((1, W))]


def rglru_prompt(proj, wts, o_buf, B, S):
    W = C_BLOCKS * LANE
    Tc = _pick(S, (512, 256, 128))
    nt = S // Tc
    xcol = (3 * A_HEADS + 4 * B_HEADS) * LANE // W
    args = (proj, proj, *wts, o_buf)
    return pl.pallas_call(
        functools.partial(_rglru_prompt_kernel, Tc=Tc),
        grid=(B, nt),
        in_specs=[pl.BlockSpec((Tc, W), lambda b, i: (b * nt + i, xcol)),
                  pl.BlockSpec((Tc, W), lambda b, i: (b * nt + i, xcol + 1))] + _rg_weight_specs(W, 2)
                 + [pl.BlockSpec(memory_space=pl.ANY)],
        out_specs=[pl.BlockSpec((Tc, W), lambda b, i: (b * nt + i, 0)),
                   pl.BlockSpec((None, 1, W), lambda b, i: (b, 0, 0))],
        out_shape=[jax.ShapeDtypeStruct(o_buf.shape, BF16), jax.ShapeDtypeStruct((B, 1, W), F32)],
        input_output_aliases={len(args) - 1: 0},
        scratch_shapes=[pltpu.VMEM((SUBLANE, W), F32), pltpu.VMEM((SUBLANE, W), F32)],
        compiler_params=_cparams(("parallel", "arbitrary")),
        name="rglru_prompt",
    )(*args)


def _rglru_sample_kernel(x_ref, gc_ref, p_ref, h0_ref, cw_ref, cb_ref, wa_ref, wx_ref, ba_ref, bx_ref, lam_ref,
                         obuf_ref, o_ref, h_ref, *, Tq):
    del obuf_ref
    x = x_ref[...]
    n = x.shape[0]
    t = lax.broadcasted_iota(jnp.int32, x.shape, 0) % Tq
    p = p_ref[...]
    cw = cw_ref[...]
    u = cb_ref[...] + x * cw[CONV_W - 1:CONV_W]
    for j in range(CONV_W - 1):
        d = CONV_W - 1 - j
        shifted = jnp.where(t >= d, pltpu.roll(x, d, 0), pltpu.roll(p, n - (Tq - d), 0))
        u = u + shifted * cw[j:j + 1]
    sp = jax.nn.softplus(-lam_ref[...])
    a, mult, iu = _rg_gates(u, wa_ref, wx_ref, ba_ref[...], bx_ref[...], sp)
    acc_a, acc_b = _scan_rows(a, mult * iu, t, int(math.log2(Tq)))
    h = acc_a * h0_ref[...] + acc_b
    h_ref[...] = h
    o_ref[...] = (h * jax.nn.gelu(gc_ref[...])).astype(BF16)


def rglru_sample(proj, wts, pbuf, h0rep, o_buf, NP, NS, Tq):
    W = C_BLOCKS * LANE
    tr = _pick(NS, (256, 128, 64, 32, 16, 8))
    xcol = (3 * A_HEADS + 4 * B_HEADS) * LANE // W
    base = NP // tr
    loc = pl.BlockSpec((tr, W), lambda i: (i, 0))
    args = (proj, proj, pbuf, h0rep, *wts, o_buf)
    return pl.pallas_call(
        functools.partial(_rglru_sample_kernel, Tq=Tq),
        grid=(NS // tr,),
        in_specs=[pl.BlockSpec((tr, W), lambda i: (base + i, xcol)),
                  pl.BlockSpec((tr, W), lambda i: (base + i, xcol + 1)), loc, loc] + _rg_weight_specs(W, 1)
                 + [pl.BlockSpec(memory_space=pl.ANY)],
        out_specs=[pl.BlockSpec((tr, W), lambda i: (base + i, 0)), loc],
        out_shape=[jax.ShapeDtypeStruct(o_buf.shape, BF16), jax.ShapeDtypeStruct((NS, W), F32)],
        input_output_aliases={len(args) - 1: 0},
        compiler_params=_cparams(("parallel",)),
        name="rglru_sample",
    )(*args)


def _out_proj_kernel(x_ref, a_ref, b_ref, c_ref, wa_ref, wb_ref, wc_ref, o_ref):
    y = jnp.dot(a_ref[...], wa_ref[...], preferred_element_type=F32)
    y = y + jnp.dot(b_ref[...], wb_ref[...], preferred_element_type=F32)
    y = y + jnp.dot(c_ref[...], wc_ref[...], preferred_element_type=F32)
    o_ref[...] = x_ref[...] + y


def out_proj(x, oa, ob, oc, w):
    n, d = x.shape
    wa_, wb_, wc_ = oa.shape[1], ob.shape[1], oc.shape[1]
    assert wb_ == wc_ and wa_ % wb_ == 0
    tm = _pick(n, (1024, 512, 256, 128))
    tn = _pick(d, (512, 256, 128))
    return pl.pallas_call(
        _out_proj_kernel,
        grid=(n // tm, d // tn),
        in_specs=[pl.BlockSpec((tm, tn), lambda i, j: (i, j)),
                  pl.BlockSpec((tm, wa_), lambda i, j: (i, 0)),
                  pl.BlockSpec((tm, wb_), lambda i, j: (i, 0)),
                  pl.BlockSpec((tm, wc_), lambda i, j: (i, 0)),
                  pl.BlockSpec((wa_, tn), lambda i, j: (0, j)),
                  pl.BlockSpec((wb_, tn), lambda i, j: (wa_ // wb_, j)),
                  pl.BlockSpec((wc_, tn), lambda i, j: (wa_ // wb_ + 1, j))],
        out_specs=pl.BlockSpec((tm, tn), lambda i, j: (i, j)),
        out_shape=jax.ShapeDtypeStruct((n, d), F32),
        compiler_params=_cparams(("parallel", "arbitrary")),
        name="out_proj",
    )(x, oa, ob, oc, w, w, w)


def _ffn_up_kernel(te_ref, nu_ref, x_ref, g_ref, wg_ref, wu_ref, o_ref, h_ref, *, do_norm):
    i = pl.program_id(0)

    @pl.when(pl.program_id(1) == 0)
    def _():
        x = x_ref[...]
        if do_norm:
            x = _rms(x, g_ref[...])
        h_ref[...] = x.astype(BF16)

    @pl.when(i < nu_ref[0])
    def _():
        h = h_ref[...]
        a = jnp.dot(h, wg_ref[...], preferred_element_type=F32)
        b = jnp.dot(h, wu_ref[...], preferred_element_type=F32)
        o_ref[...] = (jax.nn.silu(a) * b).astype(BF16)

    @pl.when(i >= nu_ref[0])
    def _():
        o_ref[...] = jnp.zeros(o_ref.shape, BF16)


def ffn_up(x, g, wg, wu, te, nu, tm, do_norm):
    n, d = x.shape
    f = wg.shape[-1]
    tf = _pick(f, tuple(t for t in (1408, 512, 256, 128) if tm * t <= 512 * 1408))
    grid_spec = pltpu.PrefetchScalarGridSpec(
        num_scalar_prefetch=2,
        grid=(n // tm, f // tf),
        in_specs=[pl.BlockSpec((tm, d), lambda i, j, te, nu: (i, 0)),
                  pl.BlockSpec((1, d), lambda i, j, te, nu: (0, 0)),
                  pl.BlockSpec((None, d, tf), lambda i, j, te, nu: (te[i], 0, j)),
                  pl.BlockSpec((None, d, tf), lambda i, j, te, nu: (te[i], 0, j))],
        out_specs=pl.BlockSpec((tm, tf), lambda i, j, te, nu: (i, j)),
        scratch_shapes=[pltpu.VMEM((tm, d), BF16)])
    return pl.pallas_call(
        functools.partial(_ffn_up_kernel, do_norm=do_norm),
        grid_spec=grid_spec,
        out_shape=jax.ShapeDtypeStruct((n, f), BF16),
        compiler_params=_cparams(("parallel", "arbitrary")),
        name="ffn_up",
    )(te, nu, x, g.reshape(1, d), wg, wu)


def _expert_changed(te_ref, i):
    return (i == 0) | (te_ref[i] != te_ref[jnp.maximum(i - 1, 0)])


def _moe_up_kernel(te_ref, nu_ref, x_ref, wg_ref, wu_ref, o_ref, wgb_ref, wub_ref):
    i = pl.program_id(1)

    @pl.when(_expert_changed(te_ref, i))
    def _():
        wgb_ref[...] = wg_ref[...].astype(BF16)
        wub_ref[...] = wu_ref[...].astype(BF16)

    @pl.when(i < nu_ref[0])
    def _():
        x = x_ref[...]
        a = jnp.dot(x, wgb_ref[...], preferred_element_type=F32)
        b = jnp.dot(x, wub_ref[...], preferred_element_type=F32)
        o_ref[...] = (jax.nn.silu(a) * b).astype(BF16)

    @pl.when(i >= nu_ref[0])
    def _():
        o_ref[...] = jnp.zeros(o_ref.shape, BF16)


def moe_up(xs, wg, wu, m, te, nu, tm):
    n, d = xs.shape
    f = wg.shape[-1]
    tf = _pick(f, (512, 256, 128))
    wspec = pl.BlockSpec((None, None, d, tf), lambda j, i, te, nu: (m, te[i], 0, j))
    grid_spec = pltpu.PrefetchScalarGridSpec(
        num_scalar_prefetch=2,
        grid=(f // tf, n // tm),
        in_specs=[pl.BlockSpec((tm, d), lambda j, i, te, nu: (i, 0)), wspec, wspec],
        out_specs=pl.BlockSpec((tm, tf), lambda j, i, te, nu: (i, j)),
        scratch_shapes=[pltpu.VMEM((d, tf), BF16), pltpu.VMEM((d, tf), BF16)])
    return pl.pallas_call(
        _moe_up_kernel,
        grid_spec=grid_spec,
        out_shape=jax.ShapeDtypeStruct((n, f), BF16),
        compiler_params=_cparams(("arbitrary", "arbitrary")),
        name="moe_up",
    )(te, nu, xs, wg, wu)


def _moe_down_kernel(te_ref, nu_ref, a_ref, wd_ref, o_ref, wdb_ref):
    i = pl.program_id(1)

    @pl.when(_expert_changed(te_ref, i))
    def _():
        wdb_ref[...] = wd_ref[...].astype(BF16)

    @pl.when(i < nu_ref[0])
    def _():
        o_ref[...] = jnp.dot(a_ref[...], wdb_ref[...], preferred_element_type=F32)

    @pl.when(i >= nu_ref[0])
    def _():
        o_ref[...] = jnp.zeros(o_ref.shape, F32)


def moe_down(act, wd, m, te, nu, tm):
    n, f = act.shape
    d = wd.shape[-1]
    tn = _pick(d, (512, 256, 128))
    grid_spec = pltpu.PrefetchScalarGridSpec(
        num_scalar_prefetch=2,
        grid=(d // tn, n // tm),
        in_specs=[pl.BlockSpec((tm, f), lambda j, i, te, nu: (i, 0)),
                  pl.BlockSpec((None, None, f, tn), lambda j, i, te, nu: (m, te[i], 0, j))],
        out_specs=pl.BlockSpec((tm, tn), lambda j, i, te, nu: (i, j)),
        scratch_shapes=[pltpu.VMEM((f, tn), BF16)])
    return pl.pallas_call(
        _moe_down_kernel,
        grid_spec=grid_spec,
        out_shape=jax.ShapeDtypeStruct((n, d), F32),
        compiler_params=_cparams(("arbitrary", "arbitrary"), vmem_mb=56),
        name="moe_down",
    )(te, nu, act, wd)


def _ffn_down_kernel(te_ref, nu_ref, a_ref, wd_ref, *rest, residual):
    i = pl.program_id(0)
    if residual:
        r_ref, o_ref = rest
    else:
        (o_ref,) = rest

    @pl.when(i < nu_ref[0])
    def _():
        y = jnp.dot(a_ref[...], wd_ref[...], preferred_element_type=F32)
        if residual:
            y = r_ref[...] + y
        o_ref[...] = y

    @pl.when(i >= nu_ref[0])
    def _():
        o_ref[...] = jnp.zeros(o_ref.shape, F32)


def ffn_down(act, wd, te, nu, tm, res):
    n, f = act.shape
    d = wd.shape[-1]
    tn = _pick(d, tuple(t for t in (1024, 512, 256, 128) if tm * t <= 512 * 1024))
    in_specs = [pl.BlockSpec((tm, f), lambda i, j, te, nu: (i, 0)),
                pl.BlockSpec((None, f, tn), lambda i, j, te, nu: (te[i], 0, j))]
    args = [act, wd]
    if res is not None:
        in_specs.append(pl.BlockSpec((tm, tn), lambda i, j, te, nu: (i, j)))
        args.append(res)
    grid_spec = pltpu.PrefetchScalarGridSpec(
        num_scalar_prefetch=2,
        grid=(n // tm, d // tn),
        in_specs=in_specs,
        out_specs=pl.BlockSpec((tm, tn), lambda i, j, te, nu: (i, j)))
    return pl.pallas_call(
        functools.partial(_ffn_down_kernel, residual=res is not None),
        grid_spec=grid_spec,
        out_shape=jax.ShapeDtypeStruct((n, d), F32),
        compiler_params=_cparams(("parallel", "arbitrary")),
        name="ffn_down",
    )(te, nu, *args)


def _router_kernel(x_ref, g_ref, wr_ref, h_ref, rt_ref):
    h = _rms(x_ref[...], g_ref[...])
    h_ref[...] = h
    logits = jnp.dot(h, wr_ref[...], precision=HI, preferred_element_type=F32)
    lane = lax.broadcasted_iota(jnp.int32, logits.shape, 1)
    l1 = jnp.where(lane < N_EXPERTS, logits, -jnp.inf)
    m1 = jnp.max(l1, axis=-1, keepdims=True)
    i1 = jnp.min(jnp.where(l1 == m1, lane, LANE), axis=-1, keepdims=True)
    l2 = jnp.where(lane == i1, -jnp.inf, l1)
    m2 = jnp.max(l2, axis=-1, keepdims=True)
    i2 = jnp.min(jnp.where(l2 == m2, lane, LANE), axis=-1, keepdims=True)
    e = jnp.exp(m2 - m1)
    g1 = 1.0 / (1.0 + e)
    g2 = e / (1.0 + e)
    rt_ref[...] = jnp.where(lane == 0, i1.astype(F32),
                            jnp.where(lane == 1, i2.astype(F32),
                                      jnp.where(lane == 2, g1, jnp.where(lane == 3, g2, 0.0))))


def route_tokens(x, g, wr):
    n, d = x.shape
    tm = _pick(n, (512, 256, 128))
    wr_pad = jnp.zeros((d, LANE), F32).at[:, :N_EXPERTS].set(wr)
    return pl.pallas_call(
        _router_kernel,
        grid=(n // tm,),
        in_specs=[pl.BlockSpec((tm, d), lambda i: (i, 0)),
                  pl.BlockSpec((1, d), lambda i: (0, 0)),
                  pl.BlockSpec((d, LANE), lambda i: (0, 0))],
        out_specs=[pl.BlockSpec((tm, d), lambda i: (i, 0)), pl.BlockSpec((tm, LANE), lambda i: (i, 0))],
        out_shape=[jax.ShapeDtypeStruct((n, d), F32), jax.ShapeDtypeStruct((n, LANE), F32)],
        compiler_params=_cparams(("parallel",)),
        name="router",
    )(x, g.reshape(1, d), wr_pad)


DISPATCH_ROWS = 256
COMBINE_ROWS = 128
GATHER_UNROLL = 8


def _gather_start(idx_ref, idx0, src_ref, dst_ref, sem, rows):
    def issue(g, c):
        for u in range(GATHER_UNROLL):
            t = g * GATHER_UNROLL + u
            pltpu.make_async_copy(src_ref.at[pl.ds(idx_ref[idx0 + t], 1)], dst_ref.at[pl.ds(t, 1)],
                                  sem).start(priority=u % 2)
        return c
    lax.fori_loop(0, rows // GATHER_UNROLL, issue, 0)


def _gather_wait(src_ref, dst_ref, sem, rows):
    pltpu.make_async_copy(src_ref.at[pl.ds(0, rows)], dst_ref, sem).wait()


def _dispatch_kernel(src_ref, nrows_ref, h_ref, o_ref, buf_ref, sem):
    i = pl.program_id(0)
    R = DISPATCH_ROWS
    live = lambda s: s * R < nrows_ref[0]

    def start(s):
        _gather_start(src_ref, s * R, h_ref, buf_ref.at[s % 2], sem.at[s % 2], R)

    @pl.when(i == 0)
    def _():
        start(i)

    @pl.when((i + 1 < pl.num_programs(0)) & live(i + 1))
    def _():
        start(i + 1)

    @pl.when(live(i))
    def _():
        _gather_wait(h_ref, buf_ref.at[i % 2], sem.at[i % 2], R)
        o_ref[...] = buf_ref[i % 2].astype(BF16)

    @pl.when(jnp.logical_not(live(i)))
    def _():
        o_ref[...] = jnp.zeros(o_ref.shape, BF16)


def dispatch(h, src, nrows):
    n, d = h.shape
    p_rows = src.shape[0]
    assert p_rows % DISPATCH_ROWS == 0
    grid_spec = pltpu.PrefetchScalarGridSpec(
        num_scalar_prefetch=2,
        grid=(p_rows // DISPATCH_ROWS,),
        in_specs=[pl.BlockSpec(memory_space=pl.ANY)],
        out_specs=pl.BlockSpec((DISPATCH_ROWS, d), lambda i, src, nr: (i, 0)),
        scratch_shapes=[pltpu.VMEM((2, DISPATCH_ROWS, d), F32), pltpu.SemaphoreType.DMA((2,))])
    return pl.pallas_call(
        _dispatch_kernel,
        grid_spec=grid_spec,
        out_shape=jax.ShapeDtypeStruct((p_rows, d), BF16),
        compiler_params=_cparams(("arbitrary",)),
        name="moe_dispatch",
    )(src, nrows, h)


def _combine_kernel(pos_ref, x_ref, rt_ref, y_ref, o_ref, buf_ref, sem, *, n):
    i = pl.program_id(0)
    R = COMBINE_ROWS

    def start(s):
        for k in range(TOP_K):
            _gather_start(pos_ref, k * n + s * R, y_ref, buf_ref.at[s % 2, k], sem.at[s % 2], R)

    @pl.when(i == 0)
    def _():
        start(i)

    @pl.when(i + 1 < pl.num_programs(0))
    def _():
        start(i + 1)

    for k in range(TOP_K):
        _gather_wait(y_ref, buf_ref.at[i % 2, k], sem.at[i % 2], R)
    rt = rt_ref[...]
    o_ref[...] = x_ref[...] + rt[:, 2:3] * buf_ref[i % 2, 0] + rt[:, 3:4] * buf_ref[i % 2, 1]


def combine(x, rt, y, pos):
    n, d = x.shape
    assert n % COMBINE_ROWS == 0
    grid_spec = pltpu.PrefetchScalarGridSpec(
        num_scalar_prefetch=1,
        grid=(n // COMBINE_ROWS,),
        in_specs=[pl.BlockSpec((COMBINE_ROWS, d), lambda i, pos: (i, 0)),
                  pl.BlockSpec((COMBINE_ROWS, LANE), lambda i, pos: (i, 0)),
                  pl.BlockSpec(memory_space=pl.ANY)],
        out_specs=pl.BlockSpec((COMBINE_ROWS, d), lambda i, pos: (i, 0)),
        scratch_shapes=[pltpu.VMEM((2, TOP_K, COMBINE_ROWS, d), F32), pltpu.SemaphoreType.DMA((2,))])
    return pl.pallas_call(
        functools.partial(_combine_kernel, n=n),
        grid_spec=grid_spec,
        out_shape=jax.ShapeDtypeStruct((n, d), F32),
        compiler_params=_cparams(("arbitrary",)),
        name="moe_combine",
    )(pos, x, rt, y)


def _route_plan(rt, tm, n_tiles):
    n = rt.shape[0]
    e_flat = jnp.concatenate([rt[:, 0], rt[:, 1]]).astype(jnp.int32)
    onehot = (e_flat[:, None] == jnp.arange(N_EXPERTS)[None, :]).astype(jnp.int32)
    csum = jnp.cumsum(onehot, axis=0)
    rank = jnp.sum(csum * onehot, axis=1) - 1
    counts = csum[-1]
    padded = ((counts + tm - 1) // tm) * tm
    gend = jnp.cumsum(padded)
    gstart = gend - padded
    pos = jnp.sum(onehot * gstart[None, :], axis=1) + rank
    nu = (gend[-1] // tm).astype(jnp.int32).reshape(1)
    tile_start = jnp.arange(n_tiles, dtype=jnp.int32) * tm
    te = jnp.sum((tile_start[:, None] >= gend[None, :]).astype(jnp.int32), axis=1)
    last = jnp.max(jnp.where(counts > 0, jnp.arange(N_EXPERTS), 0))
    te = jnp.minimum(te, last).astype(jnp.int32)
    pos = pos.astype(jnp.int32)
    tok = jnp.arange(TOP_K * n, dtype=jnp.int32) % n
    src = jnp.zeros((n_tiles * tm,), jnp.int32).at[pos].set(tok)
    return pos, src, te, nu


def ffn_moe(x, g, wr, wg, wu, wd, m):
    n, d = x.shape
    tm = _pick(n, (512, 256, 128))
    n_tiles = (TOP_K * n) // tm + N_EXPERTS
    h, rt = route_tokens(x, g, wr)
    pos, src, te, nu = _route_plan(rt, tm, n_tiles)
    xs = dispatch(h, src, nu * tm)
    act = moe_up(xs, wg, wu, m, te, nu, tm)
    y = moe_down(act, wd, m, te, nu, tm)
    return combine(x, rt, y, pos)


def ffn_dense(x, g, wg, wu, wd):
    n, d = x.shape
    tm = _pick(n, (1024, 512, 256, 128))
    te = jnp.zeros((n // tm,), jnp.int32)
    nu = jnp.full((1,), n // tm, jnp.int32)
    act = ffn_up(x, g, wg[None], wu[None], te, nu, tm, do_norm=True)
    return ffn_down(act, wd[None], te, nu, tm, x)


def kernel(x_prompt, x_sample, cache_k, cache_v, page_table, state_hgrn, state_rglru, state_conv,
           w_in, w_out, g_mix, g_ffn, q_norm, k_norm, lam_q1, lam_k1, lam_q2, lam_k2, subln,
           rel_table, hgrn_lb, hgrn_onorm, conv_w, conv_b, rg_wa, rg_ba, rg_wx, rg_bx, rg_lambda,
           ff_gate, ff_up, ff_down, router, ex_gate, ex_up, ex_down):
    B, S, D = x_prompt.shape
    DB, Tq, _ = x_sample.shape
    depth = w_in.shape[0]
    NP, NS = B * S, DB * Tq
    AW = A_HEADS * LANE
    CW = C_BLOCKS * LANE
    xc0 = 3 * AW + 4 * B_HEADS * LANE
    x = jnp.concatenate([x_prompt.reshape(NP, D), x_sample.reshape(NS, D)], axis=0)

    sm = jax.nn.softmax(hgrn_lb.astype(F32), axis=0)
    lb_all = jnp.cumsum(sm, axis=0) - sm[0:1]

    outs = {k: [] for k in ("kp", "vp", "sp", "hp", "bp", "ks", "vs", "ss", "hs", "bs")}
    for l in range(depth):
        lam_init = 0.8 - 0.6 * math.exp(-0.3 * l)
        lamp = jnp.stack([lam_q1[l], lam_k1[l], lam_q2[l], lam_k2[l]]).astype(F32)
        proj = norm_matmul(x, g_mix[l], w_in[l].astype(BF16))
        ta = _pick(math.gcd(S, NP + NS), (512, 256, 128))
        qf, kf, kb, qt, vt = qk_prep(proj, q_norm[l], k_norm[l], ta)
        vf = proj[:, 2 * AW:3 * AW]
        oa = attn_prompt(qt, kb, vt, rel_table, lamp, subln[l], jnp.zeros((NP + NS, AW), BF16), B, S, ta, lam_init)
        kn2 = kf[NP:].reshape(DB, Tq * A_HEADS, LANE)
        vn2 = vf[NP:].reshape(DB, Tq * A_HEADS, LANE)
        oa = attn_sample(qf, kn2, vn2, cache_k, cache_v, l, page_table, rel_table, lamp, subln[l], oa,
                         NP, DB, Tq, lam_init)
        ob, s_p = hgrn(proj, lb_all[l], hgrn_onorm[l], 0, B, S, None, l, jnp.zeros((NP + NS, B_HEADS * LANE), BF16))
        ob, s_s = hgrn(proj, lb_all[l], hgrn_onorm[l], NP, DB, Tq, state_hgrn, l, ob)
        wts = (conv_w[l], conv_b[l].reshape(1, CW), rg_wa[l], rg_wx[l], rg_ba[l].reshape(1, CW),
               rg_bx[l].reshape(1, CW), rg_lambda[l].reshape(1, CW))
        oc, h_p = rglru_prompt(proj, wts, jnp.zeros((NP + NS, CW), BF16), B, S)
        pbuf = jnp.concatenate([jnp.zeros((DB, Tq - (CONV_W - 1), CW), F32), state_conv[l]], axis=1).reshape(NS, CW)
        h0rep = jnp.repeat(state_rglru[l], Tq, axis=0)
        oc, h_s = rglru_sample(proj, wts, pbuf, h0rep, oc, NP, NS, Tq)
        x = out_proj(x, oa, ob, oc, w_out[l].astype(BF16))
        m = l // 2
        if l % 2 == 0:
            x = ffn_dense(x, g_ffn[l], ff_gate[m].astype(BF16), ff_up[m].astype(BF16), ff_down[m].astype(BF16))
        else:
            x = ffn_moe(x, g_ffn[l], router[m], ex_gate, ex_up, ex_down, m)
        xcs = proj[:, xc0:xc0 + CW]
        outs["kp"].append(kf[:NP].reshape(B, S, A_HEADS, LANE))
        outs["vp"].append(vf[:NP].reshape(B, S, A_HEADS, LANE))
        outs["sp"].append(s_p)
        outs["hp"].append(h_p.reshape(B, CW))
        outs["bp"].append(xcs[:NP].reshape(B, S, CW)[:, S - (CONV_W - 1):])
        outs["ks"].append(kf[NP:].reshape(DB, Tq, A_HEADS, LANE))
        outs["vs"].append(vf[NP:].reshape(DB, Tq, A_HEADS, LANE))
        outs["ss"].append(s_s)
        outs["hs"].append(h_s.reshape(DB, Tq, CW)[:, Tq - 1])
        outs["bs"].append(xcs[NP:].reshape(DB, Tq, CW)[:, Tq - (CONV_W - 1):])
    st = {k: jnp.stack(v) for k, v in outs.items()}
    return (x[:NP].reshape(B, S, D), x[NP:].reshape(DB, Tq, D),
            st["kp"], st["vp"], st["sp"], st["hp"], st["bp"],
            st["ks"], st["vs"], st["ss"], st["hs"], st["bs"])
```
